```python
import math
import jax, jax.numpy as jnp
from jax import lax
import numpy as np

D_MODEL = 1024
BATCH = 16
SEQ = 4096
DEPTH = 1
DEC_BATCH = 128
DEC_SEQ = 8
PAST_LEN = 8192
PAGE_SIZE = 128

HEAD_DIM = 64
N_HEADS = 8
N_KV_HEADS = 2
ATTN_WIDTH = N_HEADS * HEAD_DIM
KV_WIDTH = N_KV_HEADS * HEAD_DIM
ROPE_DIM = HEAD_DIM // 4
ROPE_THETA = 500000.0
IDX_HEADS = 8
IDX_DIM = 64
TOPK_MAX = 256
QBLK = 128
POOL_WIDTH = D_MODEL - ATTN_WIDTH
POOL_WINDOWS = (2, 4, 8, 16)
POOL_GROUP = POOL_WIDTH // len(POOL_WINDOWS)
POOL_STATE = max(POOL_WINDOWS) - 1
MIX_WIDTH = POOL_WIDTH + ATTN_WIDTH
IN_SPLITS = (POOL_WIDTH, ATTN_WIDTH, KV_WIDTH, KV_WIDTH, IDX_HEADS * IDX_DIM, IDX_DIM, IDX_HEADS)
IN_WIDTH = POOL_WIDTH + ATTN_WIDTH + 2 * KV_WIDTH + IDX_HEADS * IDX_DIM + IDX_DIM + IDX_HEADS
N_GROUPS = 4
EXPERTS_PER_GROUP = 8
N_EXPERTS = N_GROUPS * EXPERTS_PER_GROUP
EXPERT_TOPK = 2
EXPERT_HIDDEN = 256
MOE_BLOCK = 128
NORM_EPS = 1e-6

kernel_name = 'hybrid_pool_dsa_hmoe_step'


def rms_norm(x, g):
    xf = x.astype(jnp.float32)
    y = xf * lax.rsqrt(jnp.mean(xf * xf, axis=-1, keepdims=True) + NORM_EPS)
    return (y * g.astype(jnp.float32)).astype(x.dtype)


def rope(x, pos):
    half = ROPE_DIM // 2
    inv = jnp.power(jnp.float32(ROPE_THETA), -jnp.arange(half, dtype=jnp.float32) / half)
    ang = pos.astype(jnp.float32)[:, None] * inv[None, :]
    cos = jnp.cos(ang)[None, :, None, :]
    sin = jnp.sin(ang)[None, :, None, :]
    xr = x[..., :ROPE_DIM].astype(jnp.float32)
    x1, x2 = xr[..., :half], xr[..., half:]
    rot = jnp.concatenate([x1 * cos - x2 * sin, x2 * cos + x1 * sin], axis=-1).astype(x.dtype)
    return jnp.concatenate([rot, x[..., ROPE_DIM:]], axis=-1)


def project(xn, w_in):
    z = xn @ w_in
    B, T, _ = z.shape
    cuts = np.cumsum(np.array(IN_SPLITS))[:-1].tolist()
    u, q, k, v, qi, ki, wi = jnp.split(z, cuts, axis=-1)
    q = q.reshape(B, T, N_HEADS, HEAD_DIM)
    k = k.reshape(B, T, N_KV_HEADS, HEAD_DIM)
    v = v.reshape(B, T, N_KV_HEADS, HEAD_DIM)
    qi = qi.reshape(B, T, IDX_HEADS, IDX_DIM)
    wi = wi * (IDX_HEADS ** -0.5)
    return u, q, k, v, qi, ki, wi


def pool_mix(u_ext, pos0, w_pool, scale):
    P = POOL_STATE
    T = u_ext.shape[1] - P
    csum = jnp.cumsum(u_ext.astype(jnp.float32), axis=1)
    csum = jnp.pad(csum, ((0, 0), (1, 0), (0, 0)))
    pos = pos0 + jnp.arange(T, dtype=jnp.int32)
    cur = u_ext[:, P:].astype(jnp.float32)
    outs = []
    for g, w in enumerate(POOL_WINDOWS):
        c = slice(g * POOL_GROUP, (g + 1) * POOL_GROUP)
        win = csum[:, P + 1:P + 1 + T, c] - csum[:, P + 1 - w:P + 1 - w + T, c]
        cnt = jnp.minimum(pos + 1, w).astype(jnp.float32)[None, :, None]
        d = (win / cnt - cur[..., c]).astype(u_ext.dtype)
        outs.append(d @ w_pool[g])
    return jnp.concatenate(outs, axis=-1) * scale


def indexer_select(qi, wi, ki, qpos, kpos, topk):
    s = jnp.einsum('bthd,bsd->bths', qi, ki, preferred_element_type=jnp.float32) * (IDX_DIM ** -0.5)
    score = jnp.einsum('bths,bth->bts', jax.nn.relu(s), wi.astype(jnp.float32))
    score = jnp.where(kpos[None, None, :] <= qpos[None, :, None], score, -jnp.inf)
    _, sel = lax.top_k(score, topk)
    valid = sel <= qpos[None, :, None]
    return sel, valid


def sparse_attend(q, ksel, vsel, valid):
    B, T, H, Dh = q.shape
    qg = q.reshape(B, T, N_KV_HEADS, H // N_KV_HEADS, Dh)
    s = jnp.einsum('btkgd,btnkd->btkgn', qg, ksel, preferred_element_type=jnp.float32) * (HEAD_DIM ** -0.5)
    s = jnp.where(valid[:, :, None, None, :], s, -jnp.inf)
    p = jax.nn.softmax(s, axis=-1)
    o = jnp.einsum('btkgn,btnkd->btkgd', p.astype(vsel.dtype), vsel)
    return o.reshape(B, T, H * Dh)


def take_rows(rows, idx):
    return jax.vmap(lambda r, i: r[i])(rows, idx)


def dsa_prompt(q, k, v, qi, ki, wi):
    B, S = q.shape[:2]
    nb = S // QBLK
    topk = min(TOPK_MAX, S // 4)
    kpos = jnp.arange(S, dtype=jnp.int32)

    def one_block(args):
        qb, qib, wib, qpos = args
        sel, valid = indexer_select(qib, wib, ki, qpos, kpos, topk)
        return sparse_attend(qb, take_rows(k, sel), take_rows(v, sel), valid)

    to_blocks = lambda t: t.reshape(B, nb, QBLK, *t.shape[2:]).swapaxes(0, 1)
    out = lax.map(one_block, (to_blocks(q), to_blocks(qi), to_blocks(wi), kpos.reshape(nb, QBLK)))
    return out.swapaxes(0, 1).reshape(B, S, ATTN_WIDTH)


def gather_paged(cache, l, page_table, sel, new_rows, past_len):
    DB, T, n = sel.shape
    logical = jnp.minimum(sel, past_len - 1) // PAGE_SIZE
    phys = jnp.take_along_axis(page_table, logical.reshape(DB, T * n), axis=1).reshape(DB, T, n)
    past_rows = cache[l, phys, sel % PAGE_SIZE].astype(new_rows.dtype)
    new_sel = take_rows(new_rows, jnp.clip(sel - past_len, 0, new_rows.shape[1] - 1))
    is_past = (sel < past_len)[..., None, None]
    return jnp.where(is_past, past_rows, new_sel)


def dsa_sample(q, k, v, qi, ki, wi, cache_k, cache_v, cache_idx_k, l, page_table, past_len):
    DB, T = q.shape[:2]
    ki_past = cache_idx_k[l, page_table].reshape(DB, past_len, IDX_DIM).astype(ki.dtype)
    ki_all = jnp.concatenate([ki_past, ki], axis=1)
    qpos = past_len + jnp.arange(T, dtype=jnp.int32)
    kpos = jnp.arange(past_len + T, dtype=jnp.int32)
    topk = min(TOPK_MAX, (past_len + T) // 4)
    sel, valid = indexer_select(qi, wi, ki_all, qpos, kpos, topk)
    ksel = gather_paged(cache_k, l, page_table, sel, k, past_len)
    vsel = gather_paged(cache_v, l, page_table, sel, v, past_len)
    return sparse_attend(q, ksel, vsel, valid)


def hier_moe(h, rg_w, rg_b, re_w, re_b, w_gate, w_up, w_down):
    B, T, D = h.shape
    x = h.reshape(B * T, D)
    N = x.shape[0]
    g_logits = (x @ rg_w).astype(jnp.float32) + rg_b.astype(jnp.float32)
    g_prob = jax.nn.softmax(g_logits, axis=-1)
    g_sel = jnp.argmax(g_logits, axis=-1)
    g_w = jnp.max(g_prob, axis=-1)
    e_logits = ((x @ re_w).astype(jnp.float32) + re_b.astype(jnp.float32)).reshape(N, N_GROUPS, EXPERTS_PER_GROUP)
    e_in = e_logits[jnp.arange(N), g_sel]
    top_w, top_i = lax.top_k(jax.nn.softmax(e_in, axis=-1), EXPERT_TOPK)
    comb = g_w[:, None] * top_w / jnp.sum(top_w, axis=-1, keepdims=True)
    eid = (g_sel[:, None] * EXPERTS_PER_GROUP + top_i).reshape(-1)
    tok = jnp.repeat(jnp.arange(N, dtype=jnp.int32), EXPERT_TOPK)
    wts = comb.reshape(-1)
    M = eid.shape[0]
    order = jnp.argsort(eid)
    se, st, sw = eid[order], tok[order], wts[order]
    counts = jnp.bincount(eid, length=N_EXPERTS)
    starts = jnp.cumsum(counts) - counts
    padded = ((counts + MOE_BLOCK - 1) // MOE_BLOCK) * MOE_BLOCK
    pends = jnp.cumsum(padded)
    pstarts = pends - padded
    dest = pstarts[se] + (jnp.arange(M) - starts[se])
    n_blk = -(-M // MOE_BLOCK) + N_EXPERTS
    row_tok = jnp.full((n_blk * MOE_BLOCK,), N, jnp.int32).at[dest].set(st)
    blk_e = jnp.minimum(jnp.searchsorted(pends, jnp.arange(n_blk) * MOE_BLOCK, side='right'), N_EXPERTS - 1)
    x_pad = jnp.concatenate([x, jnp.zeros((1, D), x.dtype)], axis=0)

    def run_block(args):
        toks, e = args
        xb = x_pad[toks]
        return (jax.nn.silu(xb @ w_gate[e]) * (xb @ w_up[e])) @ w_down[e]

    out = lax.map(run_block, (row_tok.reshape(n_blk, MOE_BLOCK), blk_e)).reshape(n_blk * MOE_BLOCK, D)
    y = jnp.zeros((N, D), jnp.float32).at[st].add(out[dest].astype(jnp.float32) * sw[:, None])
    return y.astype(h.dtype).reshape(B, T, D)


def setup_inputs(seed: int = 0) -> dict:
    key = jax.random.key(seed)
    ks = jax.random.split(key, 24)
    n_pages = PAST_LEN // PAGE_SIZE
    n_pool = (DEC_BATCH * n_pages * 5) // 4
    f32 = jnp.float32
    nrm = lambda k, shape, s: jax.random.normal(k, shape, f32) * s
    page_table = jax.random.permutation(ks[6], n_pool)[:DEC_BATCH * n_pages].reshape(DEC_BATCH, n_pages).astype(jnp.int32)
    return {
        'x_prompt': nrm(ks[0], (BATCH, SEQ, D_MODEL), 1.0),
        'x_sample': nrm(ks[1], (DEC_BATCH, DEC_SEQ, D_MODEL), 1.0),
        'cache_k': nrm(ks[2], (DEPTH, n_pool, PAGE_SIZE, N_KV_HEADS, HEAD_DIM), 1.0),
        'cache_v': nrm(ks[3], (DEPTH, n_pool, PAGE_SIZE, N_KV_HEADS, HEAD_DIM), 1.0),
        'cache_idx_k': nrm(ks[4], (DEPTH, n_pool, PAGE_SIZE, IDX_DIM), 1.0),
        'state_pool': nrm(ks[5], (DEPTH, DEC_BATCH, POOL_STATE, POOL_WIDTH), 1.0),
        'page_table': page_table,
        'norm_mix_g': 1.0 + nrm(ks[7], (DEPTH, D_MODEL), 0.05),
        'w_in': nrm(ks[8], (DEPTH, D_MODEL, IN_WIDTH), D_MODEL ** -0.5),
        'pool_w': nrm(ks[9], (DEPTH, len(POOL_WINDOWS), POOL_GROUP, POOL_GROUP), POOL_GROUP ** -0.5),
        'pool_scale': 1.0 + nrm(ks[10], (DEPTH, POOL_WIDTH), 0.1),
        'w_out': nrm(ks[11], (DEPTH, MIX_WIDTH, D_MODEL), MIX_WIDTH ** -0.5),
        'norm_ffn_g': 1.0 + nrm(ks[12], (DEPTH, D_MODEL), 0.05),
        'router_group_w': nrm(ks[13], (DEPTH, D_MODEL, N_GROUPS), D_MODEL ** -0.5),
        'router_group_b': nrm(ks[14], (DEPTH, N_GROUPS), 0.01),
        'router_expert_w': nrm(ks[15], (DEPTH, D_MODEL, N_EXPERTS), D_MODEL ** -0.5),
        'router_expert_b': nrm(ks[16], (DEPTH, N_EXPERTS), 0.01),
        'expert_w_gate': nrm(ks[17], (DEPTH, N_EXPERTS, D_MODEL, EXPERT_HIDDEN), D_MODEL ** -0.5),
        'expert_w_up': nrm(ks[18], (DEPTH, N_EXPERTS, D_MODEL, EXPERT_HIDDEN), D_MODEL ** -0.5),
        'expert_w_down': nrm(ks[19], (DEPTH, N_EXPERTS, EXPERT_HIDDEN, D_MODEL), EXPERT_HIDDEN ** -0.5),
        'final_norm_g': 1.0 + nrm(ks[20], (D_MODEL,), 0.05),
    }


def reference(x_prompt, x_sample, cache_k, cache_v, cache_idx_k, state_pool, page_table,
              norm_mix_g, w_in, pool_w, pool_scale, w_out, norm_ffn_g,
              router_group_w, router_group_b, router_expert_w, router_expert_b,
              expert_w_gate, expert_w_up, expert_w_down, final_norm_g):
    past_len = page_table.shape[1] * PAGE_SIZE
    hp, hs = x_prompt, x_sample
    kp_l, vp_l, ikp_l, pp_l = [], [], [], []
    ks_l, vs_l, iks_l, ps_l = [], [], [], []
    for l in range(DEPTH):
        moe_args = (router_group_w[l], router_group_b[l], router_expert_w[l], router_expert_b[l],
                    expert_w_gate[l], expert_w_up[l], expert_w_down[l])
        B, S, _ = hp.shape
        u, q, k, v, qi, ki, wi = project(rms_norm(hp, norm_mix_g[l]), w_in[l])
        pos = jnp.arange(S, dtype=jnp.int32)
        q, k, qi = rope(q, pos), rope(k, pos), rope(qi, pos)
        ki = rope(ki[:, :, None, :], pos)[:, :, 0, :]
        u_ext = jnp.concatenate([jnp.zeros((B, POOL_STATE, POOL_WIDTH), u.dtype), u], axis=1)
        pool_o = pool_mix(u_ext, 0, pool_w[l], pool_scale[l])
        attn_o = dsa_prompt(q, k, v, qi, ki, wi)
        hp = hp + jnp.concatenate([pool_o, attn_o], axis=-1) @ w_out[l]
        hp = hp + hier_moe(rms_norm(hp, norm_ffn_g[l]), *moe_args)
        kp_l.append(k); vp_l.append(v); ikp_l.append(ki); pp_l.append(u_ext[:, -POOL_STATE:])
        T = hs.shape[1]
        u, q, k, v, qi, ki, wi = project(rms_norm(hs, norm_mix_g[l]), w_in[l])
        pos = past_len + jnp.arange(T, dtype=jnp.int32)
        q, k, qi = rope(q, pos), rope(k, pos), rope(qi, pos)
        ki = rope(ki[:, :, None, :], pos)[:, :, 0, :]
        u_ext = jnp.concatenate([state_pool[l].astype(u.dtype), u], axis=1)
        pool_o = pool_mix(u_ext, past_len, pool_w[l], pool_scale[l])
        attn_o = dsa_sample(q, k, v, qi, ki, wi, cache_k, cache_v, cache_idx_k, l, page_table, past_len)
        hs = hs + jnp.concatenate([pool_o, attn_o], axis=-1) @ w_out[l]
        hs = hs + hier_moe(rms_norm(hs, norm_ffn_g[l]), *moe_args)
        ks_l.append(k); vs_l.append(v); iks_l.append(ki); ps_l.append(u_ext[:, -POOL_STATE:])
    y_prompt = rms_norm(hp, final_norm_g)
    y_sample = rms_norm(hs, final_norm_g)
    k_prompt, v_prompt = jnp.stack(kp_l), jnp.stack(vp_l)
    idx_k_prompt, pool_prompt = jnp.stack(ikp_l), jnp.stack(pp_l)
    k_sample, v_sample = jnp.stack(ks_l), jnp.stack(vs_l)
    idx_k_sample, pool_sample = jnp.stack(iks_l), jnp.stack(ps_l)
    return (y_prompt, y_sample, k_prompt, v_prompt, idx_k_prompt, pool_prompt,
            k_sample, v_sample, idx_k_sample, pool_sample)
```

```python
import functools
import math

import numpy as np
import jax
import jax.numpy as jnp
from jax import lax
from jax.experimental import pallas as pl
from jax.experimental.pallas import tpu as pltpu

F32 = jnp.float32
BF16 = jnp.bfloat16
I32 = jnp.int32

D_MODEL = 1024
HEAD_DIM = 64
N_HEADS = 8
N_KV_HEADS = 2
ROPE_DIM = 16
ROPE_THETA = 500000.0
IDX_HEADS = 8
IDX_DIM = 64
TOPK_MAX = 256
PAGE_SIZE = 128
POOL_WIDTH = 512
POOL_WINDOWS = (2, 4, 8, 16)
POOL_GROUP = 128
POOL_STATE = 15
ATTN_WIDTH = 512
N_GROUPS = 4
EXPERTS_PER_GROUP = 8
N_EXPERTS = 32
EXPERT_HIDDEN = 256
NORM_EPS = 1e-6

LANES = 128
IN_WIDTH = 1864
IN_PAD = 1920
C_U, C_Q, C_K, C_V, C_QI, C_KW = 0, 512, 1024, 1152, 1280, 1792
PAIRS_PER_GROUP = EXPERTS_PER_GROUP * (EXPERTS_PER_GROUP - 1) // 2
N_CLASSES = N_GROUPS * PAIRS_PER_GROUP
MOE_ROWS = 128
XE_WIDTH = D_MODEL + LANES
NEG = -1e30
INT_MIN = -2 ** 31
INT_MAX = 2 ** 31 - 1
VMEM_LIMIT = 48 * 1024 * 1024


def _class_tables():
    a, b = [], []
    for g in range(N_GROUPS):
        for lo in range(EXPERTS_PER_GROUP):
            for hi in range(lo + 1, EXPERTS_PER_GROUP):
                a.append(g * EXPERTS_PER_GROUP + lo)
                b.append(g * EXPERTS_PER_GROUP + hi)
    return np.array(a, np.int32), np.array(b, np.int32)


CLASS_A, CLASS_B = _class_tables()


def _tile(n, cap):
    t = min(n, cap)
    while n % t:
        t -= 8
    return t


def _params(sem):
    return pltpu.CompilerParams(dimension_semantics=sem, vmem_limit_bytes=VMEM_LIMIT)


def _rms(x, g):
    return x * lax.rsqrt(jnp.mean(x * x, axis=-1, keepdims=True) + NORM_EPS) * g


def _dot_nt(a, b):
    return lax.dot_general(a, b, (((1,), (1,)), ((), ())), preferred_element_type=F32)


def _sort_key(score):
    bits = lax.bitcast_convert_type(score, I32)
    return jnp.where(bits < 0, bits ^ jnp.int32(INT_MAX), bits)


def _rope_tables(pos):
    half = ROPE_DIM // 2
    r = pos.shape[0]
    inv = jnp.power(jnp.float32(ROPE_THETA), -jnp.arange(half, dtype=F32) / half)
    ang = pos.astype(F32)[:, None] * inv[None, :]
    cos, sin = jnp.cos(ang), jnp.sin(ang)
    one = lambda n: jnp.ones((r, n), F32)
    zero = lambda n: jnp.zeros((r, n), F32)
    c64 = jnp.concatenate([cos, cos, one(HEAD_DIM - ROPE_DIM)], axis=1)
    s1_64 = jnp.concatenate([-sin, zero(HEAD_DIM - half)], axis=1)
    s2_64 = jnp.concatenate([zero(half), sin, zero(HEAD_DIM - ROPE_DIM)], axis=1)
    ck = jnp.concatenate([c64, jnp.full((r, IDX_HEADS), IDX_HEADS ** -0.5, F32), one(LANES - HEAD_DIM - IDX_HEADS)], axis=1)
    return jnp.concatenate([c64, c64, s1_64, s1_64, s2_64, s2_64, ck, s1_64, zero(64), s2_64, zero(64)], axis=1)


def _inproj_kernel(x_ref, g_ref, w_ref, tab_ref, u_ref, q_ref, k_ref, v_ref, qi_ref, ki_ref, kw_ref,
                   kb_ref, vb_ref, kwb_ref):
    xb = _rms(x_ref[...], g_ref[...]).astype(BF16)
    tm = xb.shape[0]
    lane = lax.broadcasted_iota(I32, (tm, LANES), 1)

    def seg(c0, c1):
        return jnp.dot(xb, w_ref[:, c0:c1], preferred_element_type=F32)

    def rope(z, t0):
        c, s1, s2 = (tab_ref[:, t0 + i * LANES:t0 + (i + 1) * LANES] for i in range(3))
        return z * c + pltpu.roll(z, LANES - ROPE_DIM // 2, 1) * s1 + pltpu.roll(z, ROPE_DIM // 2, 1) * s2

    u_ref[...] = seg(C_U, C_Q)

    for c_base, out_ref, to_group in ((C_Q, q_ref, True), (C_QI, qi_ref, False)):
        z = seg(c_base, c_base + N_HEADS * HEAD_DIM)
        for p in range(N_HEADS // 2):
            zz = rope(z[:, p * LANES:(p + 1) * LANES], 0) * (HEAD_DIM ** -0.5)
            zr = pltpu.roll(zz, HEAD_DIM, 1)
            for hh in range(2):
                h = 2 * p + hh
                tgt = (h // (N_HEADS // N_KV_HEADS)) if to_group else 0
                src = zz if hh == tgt else zr
                keep = (lane >= tgt * HEAD_DIM) & (lane < (tgt + 1) * HEAD_DIM)
                out_ref[:, h * LANES:(h + 1) * LANES] = jnp.where(keep, src, 0.0).astype(out_ref.dtype)

    k = rope(seg(C_K, C_V), 0)
    k_ref[...] = k
    kb_ref[...] = k.astype(BF16)
    v = seg(C_V, C_QI)
    v_ref[...] = v
    vb_ref[...] = v.astype(BF16)
    kw = rope(seg(C_KW, IN_PAD), 3 * LANES)
    kw_ref[...] = kw
    kwb_ref[...] = kw.astype(BF16)
    ki_ref[...] = kw[:, :IDX_DIM]


def _inproj(x, g, w, tab, q_dtype):
    n = x.shape[0]
    tm = _tile(n, 512)
    n_tab = tab.shape[0] // tm
    row = lambda width: pl.BlockSpec((tm, width), lambda i: (i, 0))
    outs = [(POOL_WIDTH, F32), (N_HEADS * LANES, q_dtype), (LANES, F32), (LANES, F32), (IDX_HEADS * LANES, q_dtype),
            (IDX_DIM, F32), (LANES, F32), (LANES, BF16), (LANES, BF16), (LANES, BF16)]
    return pl.pallas_call(
        _inproj_kernel,
        grid=(n // tm,),
        in_specs=[row(D_MODEL), pl.BlockSpec((1, D_MODEL), lambda i: (0, 0)),
                  pl.BlockSpec((D_MODEL, IN_PAD), lambda i: (0, 0)),
                  pl.BlockSpec((tm, 6 * LANES), lambda i: (i % n_tab, 0))],
        out_specs=[row(wd) for wd, _ in outs],
        out_shape=[jax.ShapeDtypeStruct((n, wd), dt) for wd, dt in outs],
        compiler_params=_params(("arbitrary",)),
    )(x, g, w, tab)


def _pool_prompt_kernel(u_ref, pw_ref, sc_ref, o_ref, ext_ref):
    t = pl.program_id(1)
    tp = u_ref.shape[1]
    halo = POOL_STATE + 1

    @pl.when(t == 0)
    def _():
        ext_ref[0:halo, :] = jnp.zeros((halo, POOL_WIDTH), F32)

    @pl.when(t > 0)
    def _():
        ext_ref[0:halo, :] = ext_ref[tp:tp + halo, :]

    ext_ref[halo:halo + tp, :] = u_ref[0]
    pos = t * tp + lax.broadcasted_iota(I32, (tp, 1), 0)
    for g, w in enumerate(POOL_WINDOWS):
        c0 = g * POOL_GROUP
        cur = ext_ref[halo:halo + tp, c0:c0 + POOL_GROUP]
        win = cur
        for j in range(1, w):
            win = win + ext_ref[halo - j:halo - j + tp, c0:c0 + POOL_GROUP]
        cnt = jnp.minimum(pos + 1, w).astype(F32)
        d = (win / cnt - cur).astype(BF16)
        o = jnp.dot(d, pw_ref[g], preferred_element_type=F32) * sc_ref[:, c0:c0 + POOL_GROUP]
        o_ref[0, :, c0:c0 + POOL_GROUP] = o.astype(o_ref.dtype)


def _pool_prompt(u, pw, sc):
    b, s, _ = u.shape
    tp = _tile(s, 512)
    return pl.pallas_call(
        _pool_prompt_kernel,
        grid=(b, s // tp),
        in_specs=[pl.BlockSpec((1, tp, POOL_WIDTH), lambda i, t: (i, t, 0)),
                  pl.BlockSpec((len(POOL_WINDOWS), POOL_GROUP, POOL_GROUP), lambda i, t: (0, 0, 0)),
                  pl.BlockSpec((1, POOL_WIDTH), lambda i, t: (0, 0))],
        out_specs=pl.BlockSpec((1, tp, POOL_WIDTH), lambda i, t: (i, t, 0)),
        out_shape=jax.ShapeDtypeStruct((b, s, POOL_WIDTH), BF16),
        scratch_shapes=[pltpu.VMEM((tp + POOL_STATE + 1, POOL_WIDTH), F32)],
        compiler_params=_params(("arbitrary", "arbitrary")),
    )(u, pw, sc)


def _pool_sample_kernel(st_ref, u_ref, pw_ref, sc_ref, o_ref, *, pos0):
    t_new = u_ref.shape[0]
    ext = [st_ref[i] for i in range(POOL_STATE)] + [u_ref[i] for i in range(t_new)]
    for g, w in enumerate(POOL_WINDOWS):
        c0 = g * POOL_GROUP
        ds = []
        for t in range(t_new):
            e = POOL_STATE + t
            cur = ext[e][:, c0:c0 + POOL_GROUP]
            win = cur
            for j in range(1, w):
                win = win + ext[e - j][:, c0:c0 + POOL_GROUP]
            cnt = float(min(pos0 + t + 1, w))
            ds.append((win / cnt - cur).astype(BF16))
        d = jnp.concatenate(ds, axis=0)
        o = jnp.dot(d, pw_ref[g], preferred_element_type=F32) * sc_ref[:, c0:c0 + POOL_GROUP]
        db = o.shape[0] // t_new
        for t in range(t_new):
            o_ref[t, :, c0:c0 + POOL_GROUP] = o[t * db:(t + 1) * db]


def _pool_sample(st_t, u_t, pw, sc, pos0):
    t_new, db, _ = u_t.shape
    return pl.pallas_call(
        functools.partial(_pool_sample_kernel, pos0=pos0),
        out_shape=jax.ShapeDtypeStruct((t_new, db, POOL_WIDTH), F32),
        compiler_params=pltpu.CompilerParams(vmem_limit_bytes=VMEM_LIMIT),
    )(st_t, u_t, pw, sc)


def _search_threshold(count_ge, shape, topk):
    def bit_body(it, t):
        cand = t + lax.shift_left(jnp.int32(1), 31 - it)
        return jnp.where(count_ge(cand) >= topk, cand, t)
    return lax.fori_loop(0, 32, bit_body, jnp.full(shape, INT_MIN, I32))


def _search_tie_index(count_eq_below, need, shape, n_bits):
    def bit_body(it, j):
        cand = j + lax.shift_left(jnp.int32(1), n_bits - 1 - it)
        return jnp.where(count_eq_below(cand) < need, cand, j)
    return lax.fori_loop(0, n_bits, bit_body, jnp.zeros(shape, I32))


def _place_heads(o_ref_store, heads, lane):
    for p in range(N_HEADS // 2):
        parts = []
        for hh in range(2):
            h = 2 * p + hh
            g = h // (N_HEADS // N_KV_HEADS)
            parts.append(heads[h] if g == hh else pltpu.roll(heads[h], HEAD_DIM, 1))
        o_ref_store(p, jnp.where(lane < HEAD_DIM, parts[0], parts[1]))


def _dsa_prompt_kernel(q_ref, qi_ref, kwq_ref, kwb_ref, kb_ref, vb_ref, o_ref, key_s, bias_s, wb_s, j_s,
                       *, topk, ck, n_bits):
    i = pl.program_id(1)
    tq = q_ref.shape[1]
    nc = (i * tq) // ck + 1
    qpos = i * tq + lax.broadcasted_iota(I32, (tq, 1), 0)
    lane = lax.broadcasted_iota(I32, (tq, ck), 1)
    hpg = N_HEADS // N_KV_HEADS

    wi = kwq_ref[0][:, IDX_DIM:IDX_DIM + IDX_HEADS]
    for h in range(IDX_HEADS):
        wb_s[h] = jnp.broadcast_to(wi[:, h:h + 1], (tq, ck))

    def score_body(c, carry):
        kc = kwb_ref[0, pl.ds(pl.multiple_of(c * ck, ck), ck), :]
        acc = jnp.zeros((tq, ck), F32)
        for h in range(IDX_HEADS):
            s = _dot_nt(qi_ref[0, :, h * LANES:(h + 1) * LANES], kc)
            acc = acc + wb_s[h] * jnp.maximum(s, 0.0)
        kpos = c * ck + lane
        score = jnp.where(kpos <= qpos, acc + 0.0, -jnp.inf)
        key_s[c] = _sort_key(score)
        return carry

    lax.fori_loop(0, nc, score_body, 0)

    def count(pred):
        def body(c, acc):
            m = jnp.where(pred(key_s[c], c * ck + lane), 1.0, 0.0)
            for s in range(ck // LANES):
                acc = acc + m[:, s * LANES:(s + 1) * LANES]
            return acc
        acc = lax.fori_loop(0, nc, body, jnp.zeros((tq, LANES), F32))
        return jnp.sum(acc, axis=1, keepdims=True)

    thr = _search_threshold(lambda cand: count(lambda k, idx: k >= cand), (tq, 1), float(topk))
    c_gt = count(lambda k, idx: k > thr)
    c_ge = count(lambda k, idx: k >= thr)
    need = float(topk) - c_gt
    j_s[...] = jnp.full((tq, 1), INT_MAX, I32)

    @pl.when(jnp.max(c_ge - c_gt - need) > 0.0)
    def _():
        j_s[...] = _search_tie_index(lambda cand: count(lambda k, idx: (k == thr) & (idx < cand)),
                                     need, (tq, 1), n_bits)

    jmax = j_s[...]

    def bias_body(c, carry):
        k = key_s[c]
        idx = c * ck + lane
        sel = (k > thr) | ((k == thr) & (idx <= jmax))
        b = jnp.where(sel & (idx <= qpos), 0.0, NEG)
        for r in range(hpg):
            bias_s[c, r * tq:(r + 1) * tq, :] = b
        return carry

    lax.fori_loop(0, nc, bias_body, 0)

    heads = [None] * N_HEADS
    for g in range(N_KV_HEADS):
        qg = jnp.concatenate([q_ref[0, :, h * LANES:(h + 1) * LANES] for h in range(g * hpg, (g + 1) * hpg)], axis=0)

        def att_body(c, carry, qg=qg):
            m, l, acc = carry
            off = pl.multiple_of(c * ck, ck)
            s = _dot_nt(qg, kb_ref[0, pl.ds(off, ck), :]) + bias_s[c]
            m_new = jnp.maximum(m, jnp.max(s, axis=1, keepdims=True))
            alpha = jnp.exp(m - m_new)
            p = jnp.exp(s - m_new)
            l = alpha * l + jnp.sum(p, axis=1, keepdims=True)
            acc = alpha * acc + jnp.dot(p.astype(BF16), vb_ref[0, pl.ds(off, ck), :], preferred_element_type=F32)
            return m_new, l, acc

        m0 = jnp.full((hpg * tq, 1), NEG, F32)
        _, l, acc = lax.fori_loop(0, nc, att_body, (m0, jnp.zeros((hpg * tq, 1), F32),
                                                    jnp.zeros((hpg * tq, LANES), F32)))
        og = acc / l
        for r in range(hpg):
            heads[g * hpg + r] = og[r * tq:(r + 1) * tq]

    lane_o = lax.broadcasted_iota(I32, (tq, LANES), 1)

    def store(p, val):
        o_ref[0, :, p * LANES:(p + 1) * LANES] = val.astype(o_ref.dtype)

    _place_heads(store, heads, lane_o)


def _dsa_prompt(q, qi, kw, kwb, kb, vb, topk):
    b, s, _ = q.shape
    tq = 128
    ck = _tile(s, 512)
    assert ck >= topk and ck % tq == 0
    n_bits = max(1, int(math.ceil(math.log2(s))) + 1)
    hpg = N_HEADS // N_KV_HEADS
    qspec = lambda width: pl.BlockSpec((1, tq, width), lambda bi, i: (bi, i, 0))
    kspec = pl.BlockSpec((1, s, LANES), lambda bi, i: (bi, 0, 0))
    return pl.pallas_call(
        functools.partial(_dsa_prompt_kernel, topk=topk, ck=ck, n_bits=n_bits),
        grid=(b, s // tq),
        in_specs=[qspec(N_HEADS * LANES), qspec(IDX_HEADS * LANES), qspec(LANES), kspec, kspec, kspec],
        out_specs=qspec(ATTN_WIDTH),
        out_shape=jax.ShapeDtypeStruct((b, s, ATTN_WIDTH), BF16),
        scratch_shapes=[pltpu.VMEM((s // ck, tq, ck), I32), pltpu.VMEM((s // ck, hpg * tq, ck), F32),
                        pltpu.VMEM((IDX_HEADS, tq, ck), F32), pltpu.VMEM((tq, 1), I32)],
        compiler_params=_params(("arbitrary", "arbitrary")),
    )(q, qi, kw, kwb, kb, vb)


def _dsa_sample_kernel(pt_ref, q_ref, qi_ref, kw_ref, k_ref, v_ref, cik_hbm, ck_hbm, cv_hbm, o_ref,
                       ki_buf, k_buf, v_buf, sems, j_s, *, n_pages, topk, n_bits):
    b = pl.program_id(0)
    nb = pl.num_programs(0)
    slot = b % 2
    t_new = q_ref.shape[0]
    past = n_pages * PAGE_SIZE
    srcs = (cik_hbm, ck_hbm, cv_hbm)
    bufs = (ki_buf, k_buf, v_buf)

    def page_copy(a, page, sl, p):
        return pltpu.make_async_copy(srcs[a].at[page], bufs[a].at[sl, p], sems.at[a, sl])

    def start_all(bb, sl):
        def body(p, carry):
            page = pt_ref[bb * n_pages + p]
            for a in range(3):
                page_copy(a, page, sl, p).start()
            return carry
        lax.fori_loop(0, n_pages, body, 0)

    def wait_all(sl):
        def body(p, carry):
            for a in range(3):
                page_copy(a, 0, sl, p).wait()
            return carry
        lax.fori_loop(0, n_pages, body, 0)

    @pl.when(b == 0)
    def _():
        start_all(0, 0)

    @pl.when(b + 1 < nb)
    def _():
        start_all(b + 1, 1 - slot)

    wait_all(slot)

    pad_rows = LANES - t_new
    tq = lax.broadcasted_iota(I32, (t_new, 1), 0)
    lane_p = lax.broadcasted_iota(I32, (t_new, past), 1)
    lane_n = lax.broadcasted_iota(I32, (t_new, LANES), 1)
    rep = lambda x: jnp.concatenate([x] * N_HEADS, axis=0)

    qi = jnp.concatenate([qi_ref[:, h * LANES:h * LANES + IDX_DIM] for h in range(IDX_HEADS)], axis=0).astype(BF16)
    wi = kw_ref[:, IDX_DIM:IDX_DIM + IDX_HEADS]
    wcol = jnp.concatenate([wi[:, h:h + 1] for h in range(IDX_HEADS)], axis=0)
    ki_past = ki_buf[slot].reshape(past, IDX_DIM).astype(BF16)
    ki_new = jnp.concatenate([kw_ref[:, :IDX_DIM], jnp.zeros((pad_rows, IDX_DIM), F32)], axis=0).astype(BF16)

    def head_sum(r):
        acc = r[0:t_new]
        for h in range(1, IDX_HEADS):
            acc = acc + r[h * t_new:(h + 1) * t_new]
        return acc + 0.0

    score_p = head_sum(wcol * jnp.maximum(_dot_nt(qi, ki_past), 0.0))
    score_n = jnp.where(lane_n <= tq, head_sum(wcol * jnp.maximum(_dot_nt(qi, ki_new), 0.0)), -jnp.inf)
    key_p = _sort_key(score_p)
    key_n = _sort_key(score_n)
    idx_n = past + lane_n

    def count(pred):
        cp = jnp.sum(jnp.where(pred(key_p, lane_p), 1.0, 0.0), axis=1, keepdims=True)
        return cp + jnp.sum(jnp.where(pred(key_n, idx_n), 1.0, 0.0), axis=1, keepdims=True)

    thr = _search_threshold(lambda cand: count(lambda k, idx: k >= cand), (t_new, 1), float(topk))
    c_gt = count(lambda k, idx: k > thr)
    c_ge = count(lambda k, idx: k >= thr)
    need = float(topk) - c_gt
    j_s[...] = jnp.full((t_new, 1), INT_MAX, I32)

    @pl.when(jnp.max(c_ge - c_gt - need) > 0.0)
    def _():
        j_s[...] = _search_tie_index(lambda cand: count(lambda k, idx: (k == thr) & (idx < cand)),
                                     need, (t_new, 1), n_bits)

    jmax = j_s[...]
    sel = lambda k, idx: (k > thr) | ((k == thr) & (idx <= jmax))
    bias_p = jnp.where(sel(key_p, lane_p), 0.0, NEG)
    bias_n = jnp.where(sel(key_n, idx_n) & (lane_n <= tq), 0.0, NEG)

    q = jnp.concatenate([q_ref[:, h * LANES:(h + 1) * LANES] for h in range(N_HEADS)], axis=0).astype(BF16)
    k_past = k_buf[slot].reshape(past, LANES).astype(BF16)
    v_past = v_buf[slot].reshape(past, LANES).astype(BF16)
    zpad = jnp.zeros((pad_rows, LANES), F32)
    k_new = jnp.concatenate([k_ref[...], zpad], axis=0).astype(BF16)
    v_new = jnp.concatenate([v_ref[...], zpad], axis=0).astype(BF16)
    s_p = _dot_nt(q, k_past) + rep(bias_p)
    s_n = _dot_nt(q, k_new) + rep(bias_n)
    m = jnp.maximum(jnp.max(s_p, axis=1, keepdims=True), jnp.max(s_n, axis=1, keepdims=True))
    p_p = jnp.exp(s_p - m)
    p_n = jnp.exp(s_n - m)
    l = jnp.sum(p_p, axis=1, keepdims=True) + jnp.sum(p_n, axis=1, keepdims=True)
    o = (jnp.dot(p_p.astype(BF16), v_past, preferred_element_type=F32)
         + jnp.dot(p_n.astype(BF16), v_new, preferred_element_type=F32)) / l
    heads = [o[h * t_new:(h + 1) * t_new] for h in range(N_HEADS)]

    def store(p, val):
        o_ref[:, p * LANES:(p + 1) * LANES] = val

    _place_heads(store, heads, lane_n)


def _dsa_sample(page_table, q, qi, kw, k, v, cik, ck, cv, topk):
    db, n_pages = page_table.shape
    t_new = q.shape[0] // db
    past = n_pages * PAGE_SIZE
    n_bits = max(1, int(math.ceil(math.log2(past + LANES))) + 1)
    row = lambda width: pl.BlockSpec((t_new, width), lambda b, pt: (b, 0))
    any_spec = pl.BlockSpec(memory_space=pl.ANY)
    return pl.pallas_call(
        functools.partial(_dsa_sample_kernel, n_pages=n_pages, topk=topk, n_bits=n_bits),
        grid_spec=pltpu.PrefetchScalarGridSpec(
            num_scalar_prefetch=1,
            grid=(db,),
            in_specs=[row(N_HEADS * LANES), row(IDX_HEADS * LANES), row(LANES), row(LANES), row(LANES),
                      any_spec, any_spec, any_spec],
            out_specs=row(ATTN_WIDTH),
            scratch_shapes=[pltpu.VMEM((2, n_pages, PAGE_SIZE, IDX_DIM), F32),
                            pltpu.VMEM((2, n_pages, PAGE_SIZE, LANES), F32),
                            pltpu.VMEM((2, n_pages, PAGE_SIZE, LANES), F32),
                            pltpu.SemaphoreType.DMA((3, 2)),
                            pltpu.VMEM((t_new, 1), I32)]),
        out_shape=jax.ShapeDtypeStruct((db * t_new, ATTN_WIDTH), F32),
        compiler_params=pltpu.CompilerParams(dimension_semantics=("arbitrary",), vmem_limit_bytes=56 * 1024 * 1024),
    )(page_table.reshape(-1), q, qi, kw, k, v, cik, ck, cv)


def _outproj_router_kernel(pool_ref, attn_ref, x_ref, wo_ref, g_ref, wr_ref, br_ref, cin_ref,
                           h_ref, xe_ref, meta_ref, cnt_ref, carry_s):
    i = pl.program_id(0)
    tm = x_ref.shape[0]

    @pl.when(i == 0)
    def _():
        carry_s[...] = cin_ref[...]

    h1 = (x_ref[...]
          + jnp.dot(pool_ref[...].astype(BF16), wo_ref[0:POOL_WIDTH, :], preferred_element_type=F32)
          + jnp.dot(attn_ref[...].astype(BF16), wo_ref[POOL_WIDTH:, :], preferred_element_type=F32))
    h_ref[...] = h1
    xn = _rms(h1, g_ref[...])
    xe_ref[:, 0:D_MODEL] = xn
    logits = jnp.dot(xn.astype(BF16), wr_ref[...], preferred_element_type=F32) + br_ref[...]

    lane = lax.broadcasted_iota(I32, (tm, LANES), 1).astype(F32)
    rmax = lambda v: jnp.max(v, axis=1, keepdims=True)
    rmin = lambda v: jnp.min(v, axis=1, keepdims=True)
    rsum = lambda v: jnp.sum(v, axis=1, keepdims=True)

    is_g = lane < N_GROUPS
    gl = jnp.where(is_g, logits, -jnp.inf)
    gmax = rmax(gl)
    g_sel = rmin(jnp.where(gl == gmax, lane, float(LANES)))
    g_w = 1.0 / rsum(jnp.where(is_g, jnp.exp(gl - gmax), 0.0))
    e0 = N_GROUPS + g_sel * EXPERTS_PER_GROUP
    is_e = (lane >= e0) & (lane < e0 + EXPERTS_PER_GROUP)
    el = jnp.where(is_e, logits, -jnp.inf)
    un = jnp.where(is_e, jnp.exp(el - rmax(el)), 0.0)
    prob = jnp.where(is_e, un / rsum(un), -1.0)
    p0 = rmax(prob)
    i0 = rmin(jnp.where(prob == p0, lane, float(LANES)))
    prob1 = jnp.where(lane == i0, -1.0, prob)
    p1 = rmax(prob1)
    i1 = rmin(jnp.where(prob1 == p1, lane, float(LANES)))
    w0 = g_w * p0 / (p0 + p1)
    w1 = g_w * p1 / (p0 + p1)
    lo = jnp.minimum(i0, i1) - e0
    hi = jnp.maximum(i0, i1) - e0
    cls = g_sel * PAIRS_PER_GROUP + lo * (2 * EXPERTS_PER_GROUP - 1 - lo) * 0.5 + hi - lo - 1.0
    w_a = jnp.where(i0 < i1, w0, w1)
    w_b = jnp.where(i0 < i1, w1, w0)
    xe_ref[:, D_MODEL:] = jnp.where(lane == 0.0, w_a, jnp.where(lane == 1.0, w_b, 0.0))

    onehot = lane == cls
    tri = (lax.broadcasted_iota(I32, (tm, tm), 1) < lax.broadcasted_iota(I32, (tm, tm), 0)).astype(BF16)
    before = jnp.dot(tri, onehot.astype(BF16), preferred_element_type=F32) + carry_s[...]
    rank = rsum(jnp.where(onehot, before, 0.0))
    carry_s[...] = carry_s[...] + jnp.sum(onehot.astype(F32), axis=0, keepdims=True)
    cnt_ref[...] = carry_s[...]
    meta = jnp.where(lane == 0.0, cls, jnp.where(lane == 1.0, rank, 0.0))
    meta_ref[...] = jnp.transpose(meta)[0:8, :].astype(I32)


def _outproj_router(pool_o, attn_o, x, wo, g, wr, br, counts_in):
    n = x.shape[0]
    tm = _tile(n, 512)
    row = lambda width: pl.BlockSpec((tm, width), lambda i: (i, 0))
    full = lambda shape: pl.BlockSpec(shape, lambda i: (0,) * len(shape))
    return pl.pallas_call(
        _outproj_router_kernel,
        grid=(n // tm,),
        in_specs=[row(POOL_WIDTH), row(ATTN_WIDTH), row(D_MODEL), full((D_MODEL, D_MODEL)), full((1, D_MODEL)),
                  full((D_MODEL, LANES)), full((1, LANES)), full((1, LANES))],
        out_specs=[row(D_MODEL), row(XE_WIDTH), pl.BlockSpec((8, tm), lambda i: (0, i)), full((1, LANES))],
        out_shape=[jax.ShapeDtypeStruct((n, D_MODEL), F32), jax.ShapeDtypeStruct((n, XE_WIDTH), F32),
                   jax.ShapeDtypeStruct((8, n), I32), jax.ShapeDtypeStruct((1, LANES), F32)],
        scratch_shapes=[pltpu.VMEM((1, LANES), F32)],
        compiler_params=_params(("arbitrary",)),
    )(pool_o, attn_o, x, wo, g, wr, br, counts_in)


def _load_dest(ps_ref, cls_hbm, rank_hbm, cls_s, rank_s, sem, base, tm):
    c1 = pltpu.make_async_copy(cls_hbm.at[pl.ds(base, tm)], cls_s, sem.at[0])
    c2 = pltpu.make_async_copy(rank_hbm.at[pl.ds(base, tm)], rank_s, sem.at[1])
    c1.start()
    c2.start()
    c1.wait()
    c2.wait()
    return lambda r: ps_ref[cls_s[r]] + rank_s[r]


def _dispatch_kernel(ps_ref, cls_hbm, rank_hbm, xe_hbm, xs_in_hbm, xs_hbm, cls_s, rank_s, sem, *, tm):
    del xs_in_hbm
    base = pl.program_id(0) * tm
    dest = _load_dest(ps_ref, cls_hbm, rank_hbm, cls_s, rank_s, sem, base, tm)

    def row_copy(r, d):
        return pltpu.make_async_copy(xe_hbm.at[pl.ds(base + r, 1)], xs_hbm.at[pl.ds(d, 1)], sem.at[2])

    def start(r, carry):
        row_copy(r, dest(r)).start()
        return carry

    def wait(r, carry):
        row_copy(r, 0).wait()
        return carry

    lax.fori_loop(0, tm, start, 0)
    lax.fori_loop(0, tm, wait, 0)


def _dispatch(pstart, cls, rank, xe, xs):
    n = xe.shape[0]
    tm = _tile(n, 512)
    any_spec = pl.BlockSpec(memory_space=pl.ANY)
    return pl.pallas_call(
        functools.partial(_dispatch_kernel, tm=tm),
        grid_spec=pltpu.PrefetchScalarGridSpec(
            num_scalar_prefetch=1, grid=(n // tm,),
            in_specs=[any_spec] * 4, out_specs=any_spec,
            scratch_shapes=[pltpu.SMEM((tm,), I32), pltpu.SMEM((tm,), I32), pltpu.SemaphoreType.DMA((3,))]),
        out_shape=jax.ShapeDtypeStruct(xs.shape, xs.dtype),
        input_output_aliases={4: 0},
        compiler_params=pltpu.CompilerParams(dimension_semantics=("arbitrary",), has_side_effects=True),
    )(pstart, cls, rank, xe, xs)


def _ffn_kernel(ba_ref, bb_ref, nu_ref, xs_ref, wga_ref, wua_ref, wda_ref, wgb_ref, wub_ref, wdb_ref, ys_ref):
    j = pl.program_id(0)

    @pl.when(j < nu_ref[0])
    def _():
        x = xs_ref[:, 0:D_MODEL].astype(BF16)

        def expert(wg_ref, wu_ref, wd_ref):
            gate = jnp.dot(x, wg_ref[0], preferred_element_type=F32)
            up = jnp.dot(x, wu_ref[0], preferred_element_type=F32)
            hid = (gate / (1.0 + jnp.exp(-gate))) * up
            return jnp.dot(hid.astype(BF16), wd_ref[0], preferred_element_type=F32)

        ys_ref[...] = (xs_ref[:, D_MODEL:D_MODEL + 1] * expert(wga_ref, wua_ref, wda_ref)
                       + xs_ref[:, D_MODEL + 1:D_MODEL + 2] * expert(wgb_ref, wub_ref, wdb_ref))

    @pl.when(j >= nu_ref[0])
    def _():
        ys_ref[...] = jnp.zeros(ys_ref.shape, F32)


def _ffn(blk_a, blk_b, n_used, xs, wg, wu, wd):
    n_blk = xs.shape[0] // MOE_ROWS
    up_a = pl.BlockSpec((1, D_MODEL, EXPERT_HIDDEN), lambda j, a, b, nu: (a[j], 0, 0))
    up_b = pl.BlockSpec((1, D_MODEL, EXPERT_HIDDEN), lambda j, a, b, nu: (b[j], 0, 0))
    dn_a = pl.BlockSpec((1, EXPERT_HIDDEN, D_MODEL), lambda j, a, b, nu: (a[j], 0, 0))
    dn_b = pl.BlockSpec((1, EXPERT_HIDDEN, D_MODEL), lambda j, a, b, nu: (b[j], 0, 0))
    return pl.pallas_call(
        _ffn_kernel,
        grid_spec=pltpu.PrefetchScalarGridSpec(
            num_scalar_prefetch=3, grid=(n_blk,),
            in_specs=[pl.BlockSpec((MOE_ROWS, XE_WIDTH), lambda j, a, b, nu: (j, 0)), up_a, up_a, dn_a, up_b, up_b, dn_b],
            out_specs=pl.BlockSpec((MOE_ROWS, D_MODEL), lambda j, a, b, nu: (j, 0))),
        out_shape=jax.ShapeDtypeStruct((xs.shape[0], D_MODEL), F32),
        compiler_params=_params(("arbitrary",)),
    )(blk_a, blk_b, n_used, xs, wg, wu, wd, wg, wu, wd)


def _combine_kernel(ps_ref, cls_hbm, rank_hbm, ys_hbm, h_ref, g_ref, y_ref, cls_s, rank_s, ybuf, sem, *, tm):
    base = pl.program_id(0) * tm
    dest = _load_dest(ps_ref, cls_hbm, rank_hbm, cls_s, rank_s, sem, base, tm)

    def row_copy(r, d):
        return pltpu.make_async_copy(ys_hbm.at[pl.ds(d, 1)], ybuf.at[pl.ds(r, 1)], sem.at[2])

    def start(r, carry):
        row_copy(r, dest(r)).start()
        return carry

    def wait(r, carry):
        row_copy(r, 0).wait()
        return carry

    lax.fori_loop(0, tm, start, 0)
    lax.fori_loop(0, tm, wait, 0)
    y_ref[...] = _rms(h_ref[...] + ybuf[...], g_ref[...])


def _combine(pstart, cls, rank, ys, h1, g):
    n = h1.shape[0]
    tm = _tile(n, 512)
    any_spec = pl.BlockSpec(memory_space=pl.ANY)
    return pl.pallas_call(
        functools.partial(_combine_kernel, tm=tm),
        grid_spec=pltpu.PrefetchScalarGridSpec(
            num_scalar_prefetch=1, grid=(n // tm,),
            in_specs=[any_spec, any_spec, any_spec, pl.BlockSpec((tm, D_MODEL), lambda i, ps: (i, 0)),
                      pl.BlockSpec((1, D_MODEL), lambda i, ps: (0, 0))],
            out_specs=pl.BlockSpec((tm, D_MODEL), lambda i, ps: (i, 0)),
            scratch_shapes=[pltpu.SMEM((tm,), I32), pltpu.SMEM((tm,), I32), pltpu.VMEM((tm, D_MODEL), F32),
                            pltpu.SemaphoreType.DMA((3,))]),
        out_shape=jax.ShapeDtypeStruct((n, D_MODEL), F32),
        compiler_params=_params(("arbitrary",)),
    )(pstart, cls, rank, ys, h1, g)


def _moe_and_final(pool_o, attn_o, x, wo, g_ffn, wr, br, wg, wu, wd, g_final):
    n = x.shape[0]
    h1, xe, meta, counts = _outproj_router(pool_o, attn_o, x, wo, g_ffn, wr, br, jnp.zeros((1, LANES), F32))
    cnt = counts[0].astype(I32)
    padded = ((cnt + MOE_ROWS - 1) // MOE_ROWS) * MOE_ROWS
    pends = jnp.cumsum(padded)
    pstart = (pends - padded).astype(I32)
    n_blk = -(-n // MOE_ROWS) + N_CLASSES
    n_used = (pends[-1] // MOE_ROWS).astype(I32)
    blk = jnp.arange(n_blk, dtype=I32)
    blk_cls = jnp.minimum(jnp.searchsorted(pends, blk * MOE_ROWS, side='right'), N_CLASSES - 1).astype(I32)
    blk_cls = jnp.where(blk < n_used, blk_cls, blk_cls[jnp.maximum(n_used - 1, 0)])
    blk_a = jnp.asarray(CLASS_A)[blk_cls]
    blk_b = jnp.asarray(CLASS_B)[blk_cls]
    cls, rank = meta[0], meta[1]
    xs = _dispatch(pstart, cls, rank, xe, jnp.zeros((n_blk * MOE_ROWS, XE_WIDTH), F32))
    ys = _ffn(blk_a, blk_b, n_used.reshape(1), xs, wg, wu, wd)
    return _combine(pstart, cls, rank, ys, h1, g_final)


def kernel(x_prompt, x_sample, cache_k, cache_v, cache_idx_k, state_pool, page_table, norm_mix_g, w_in, pool_w,
           pool_scale, w_out, norm_ffn_g, router_group_w, router_group_b, router_expert_w, router_expert_b,
           expert_w_gate, expert_w_up, expert_w_down, final_norm_g):
    depth = w_in.shape[0]
    assert depth == 1, "single-layer step"
    b, s, d = x_prompt.shape
    db, t_new, _ = x_sample.shape
    n_pages = page_table.shape[1]
    past = n_pages * PAGE_SIZE
    n_pool = cache_k.shape[1]
    l = 0

    w_in_b = jnp.pad(w_in[l], ((0, 0), (0, IN_PAD - IN_WIDTH))).astype(BF16)
    g_mix = norm_mix_g[l].reshape(1, d)
    pw = pool_w[l].astype(BF16)
    psc = pool_scale[l].reshape(1, POOL_WIDTH)
    wo = w_out[l].astype(BF16)
    g_ffn = norm_ffn_g[l].reshape(1, d)
    wr = jnp.pad(jnp.concatenate([router_group_w[l], router_expert_w[l]], axis=1),
                 ((0, 0), (0, LANES - N_GROUPS - N_EXPERTS))).astype(BF16)
    br = jnp.pad(jnp.concatenate([router_group_b[l], router_expert_b[l]]), (0, LANES - N_GROUPS - N_EXPERTS)).reshape(1, LANES)
    wg = expert_w_gate[l].astype(BF16)
    wu = expert_w_up[l].astype(BF16)
    wd = expert_w_down[l].astype(BF16)
    g_fin = final_norm_g.reshape(1, d)
    moe = lambda pool_o, attn_o, x: _moe_and_final(pool_o, attn_o, x, wo, g_ffn, wr, br, wg, wu, wd, g_fin)

    xp = x_prompt.reshape(b * s, d)
    tab_p = _rope_tables(jnp.arange(s, dtype=I32))
    u, q, k, v, qi, ki, kw, kb, vb, kwb = _inproj(xp, g_mix, w_in_b, tab_p, BF16)
    r3 = lambda a: a.reshape(b, s, a.shape[-1])
    pool_o = _pool_prompt(r3(u), pw, psc)
    attn_o = _dsa_prompt(r3(q), r3(qi), r3(kw), r3(kwb), r3(kb), r3(vb), min(TOPK_MAX, s // 4))
    y_prompt = moe(pool_o.reshape(b * s, POOL_WIDTH), attn_o.reshape(b * s, ATTN_WIDTH), xp).reshape(b, s, d)
    k_prompt = k.reshape(1, b, s, N_KV_HEADS, HEAD_DIM)
    v_prompt = v.reshape(1, b, s, N_KV_HEADS, HEAD_DIM)
    idx_k_prompt = ki.reshape(1, b, s, IDX_DIM)
    if s >= POOL_STATE:
        pool_prompt = r3(u)[:, s - POOL_STATE:][None]
    else:
        pool_prompt = jnp.concatenate([jnp.zeros((b, POOL_STATE, POOL_WIDTH), F32), r3(u)], axis=1)[:, -POOL_STATE:][None]

    xs_ = x_sample.reshape(db * t_new, d)
    tab_s = _rope_tables(past + (jnp.arange(db * t_new, dtype=I32) % t_new))
    u2, q2, k2, v2, qi2, ki2, kw2, _, _, _ = _inproj(xs_, g_mix, w_in_b, tab_s, F32)
    u2_3 = u2.reshape(db, t_new, POOL_WIDTH)
    pool2 = _pool_sample(jnp.swapaxes(state_pool[l], 0, 1), jnp.swapaxes(u2_3, 0, 1), pw, psc, past)
    pool2 = jnp.swapaxes(pool2, 0, 1).reshape(db * t_new, POOL_WIDTH)
    attn2 = _dsa_sample(page_table, q2, qi2, kw2, k2, v2,
                        cache_idx_k[l], cache_k[l].reshape(n_pool, PAGE_SIZE, LANES),
                        cache_v[l].reshape(n_pool, PAGE_SIZE, LANES), min(TOPK_MAX, (past + t_new) // 4))
    y_sample = moe(pool2, attn2, xs_).reshape(db, t_new, d)
    k_sample = k2.reshape(1, db, t_new, N_KV_HEADS, HEAD_DIM)
    v_sample = v2.reshape(1, db, t_new, N_KV_HEADS, HEAD_DIM)
    idx_k_sample = ki2.reshape(1, db, t_new, IDX_DIM)
    pool_sample = jnp.concatenate([state_pool[l], u2_3], axis=1)[:, -POOL_STATE:][None]

    return (y_prompt, y_sample, k_prompt, v_prompt, idx_k_prompt, pool_prompt,
            k_sample, v_sample, idx_k_sample, pool_sample)
```

```python
import functools
import math

import numpy as np
import jax
import jax.numpy as jnp
from jax import lax
from jax.experimental import pallas as pl
from jax.experimental.pallas import tpu as pltpu

F32 = jnp.float32
BF16 = jnp.bfloat16
I32 = jnp.int32

D_MODEL = 1024
HEAD_DIM = 64
N_HEADS = 8
N_KV_HEADS = 2
ROPE_DIM = 16
ROPE_THETA = 500000.0
IDX_HEADS = 8
IDX_DIM = 64
TOPK_MAX = 256
PAGE_SIZE = 128
POOL_WIDTH = 512
POOL_WINDOWS = (2, 4, 8, 16)
POOL_GROUP = 128
POOL_STATE = 15
ATTN_WIDTH = 512
N_GROUPS = 4
EXPERTS_PER_GROUP = 8
N_EXPERTS = 32
EXPERT_HIDDEN = 256
NORM_EPS = 1e-6

LANES = 128
IN_WIDTH = 1864
IN_PAD = 1920
C_U, C_Q, C_K, C_V, C_QI, C_KW = 0, 512, 1024, 1152, 1280, 1792
PAIRS_PER_GROUP = EXPERTS_PER_GROUP * (EXPERTS_PER_GROUP - 1) // 2
N_CLASSES = N_GROUPS * PAIRS_PER_GROUP
MOE_ROWS = 128
ROW_TILES = D_MODEL // LANES
XE_PITCH = ROW_TILES + 1
NEG = -1e30
INT_MIN = -2 ** 31
INT_MAX = 2 ** 31 - 1
VMEM_LIMIT = 48 * 1024 * 1024


def _class_tables():
    a, b = [], []
    for g in range(N_GROUPS):
        for lo in range(EXPERTS_PER_GROUP):
            for hi in range(lo + 1, EXPERTS_PER_GROUP):
                a.append(g * EXPERTS_PER_GROUP + lo)
                b.append(g * EXPERTS_PER_GROUP + hi)
    return np.array(a, np.int32), np.array(b, np.int32)


CLASS_A, CLASS_B = _class_tables()


def _tile(n, cap):
    t = min(n, cap)
    while n % t:
        t -= 8
    return t


def _params(sem):
    return pltpu.CompilerParams(dimension_semantics=sem, vmem_limit_bytes=VMEM_LIMIT)


def _rms(x, g):
    return x * lax.rsqrt(jnp.mean(x * x, axis=-1, keepdims=True) + NORM_EPS) * g


def _dot_nt(a, b):
    return lax.dot_general(a, b, (((1,), (1,)), ((), ())), preferred_element_type=F32)


def _fold_rows(x, op):
    parts = [x[r:r + 8] for r in range(0, x.shape[0], 8)]
    while len(parts) > 1:
        nxt = [op(parts[a], parts[a + 1]) for a in range(0, len(parts) - 1, 2)]
        if len(parts) % 2:
            nxt.append(parts[-1])
        parts = nxt
    return parts[0]


def _sort_key(score):
    bits = lax.bitcast_convert_type(score, I32)
    return jnp.where(bits < 0, bits ^ jnp.int32(INT_MAX), bits)


def _rope_tables(pos):
    half = ROPE_DIM // 2
    r = pos.shape[0]
    inv = jnp.power(jnp.float32(ROPE_THETA), -jnp.arange(half, dtype=F32) / half)
    ang = pos.astype(F32)[:, None] * inv[None, :]
    cos, sin = jnp.cos(ang), jnp.sin(ang)
    one = lambda n: jnp.ones((r, n), F32)
    zero = lambda n: jnp.zeros((r, n), F32)
    c64 = jnp.concatenate([cos, cos, one(HEAD_DIM - ROPE_DIM)], axis=1)
    s1_64 = jnp.concatenate([-sin, zero(HEAD_DIM - half)], axis=1)
    s2_64 = jnp.concatenate([zero(half), sin, zero(HEAD_DIM - ROPE_DIM)], axis=1)
    ck = jnp.concatenate([c64, jnp.full((r, IDX_HEADS), IDX_HEADS ** -0.5, F32), one(LANES - HEAD_DIM - IDX_HEADS)], axis=1)
    return jnp.concatenate([c64, c64, s1_64, s1_64, s2_64, s2_64, ck, s1_64, zero(64), s2_64, zero(64)], axis=1)


def _inproj_kernel(x_ref, g_ref, w_ref, tab_ref, u_ref, q_ref, k_ref, v_ref, qi_ref, ki_ref, kw_ref,
                   kb_ref, vb_ref, kwb_ref, *, transposed):
    xb = _rms(x_ref[...], g_ref[...]).astype(BF16)
    tm = xb.shape[0]
    lane = lax.broadcasted_iota(I32, (tm, LANES), 1)

    def seg(c0, c1):
        return jnp.dot(xb, w_ref[:, c0:c1], preferred_element_type=F32)

    def rope(z, t0):
        c, s1, s2 = (tab_ref[:, t0 + i * LANES:t0 + (i + 1) * LANES] for i in range(3))
        return z * c + pltpu.roll(z, LANES - ROPE_DIM // 2, 1) * s1 + pltpu.roll(z, ROPE_DIM // 2, 1) * s2

    u_ref[...] = seg(C_U, C_Q)

    for c_base, out_ref, to_group in ((C_Q, q_ref, True), (C_QI, qi_ref, False)):
        z = seg(c_base, c_base + N_HEADS * HEAD_DIM)
        for p in range(N_HEADS // 2):
            zz = rope(z[:, p * LANES:(p + 1) * LANES], 0) * (HEAD_DIM ** -0.5)
            zr = pltpu.roll(zz, HEAD_DIM, 1)
            for hh in range(2):
                h = 2 * p + hh
                tgt = (h // (N_HEADS // N_KV_HEADS)) if to_group else 0
                src = zz if hh == tgt else zr
                keep = (lane >= tgt * HEAD_DIM) & (lane < (tgt + 1) * HEAD_DIM)
                val = jnp.where(keep, src, 0.0)
                if transposed:
                    for qb in range(tm // LANES):
                        blk = jnp.transpose(val[qb * LANES:(qb + 1) * LANES])
                        out_ref[0, qb, :, h * LANES:(h + 1) * LANES] = blk.astype(out_ref.dtype)
                else:
                    out_ref[:, h * LANES:(h + 1) * LANES] = val.astype(out_ref.dtype)

    k = rope(seg(C_K, C_V), 0)
    k_ref[...] = k
    kb_ref[...] = k.astype(BF16)
    v = seg(C_V, C_QI)
    v_ref[...] = v
    if transposed:
        for qb in range(tm // LANES):
            vb_ref[0, 0, :, qb * LANES:(qb + 1) * LANES] = jnp.transpose(v[qb * LANES:(qb + 1) * LANES]).astype(BF16)
    else:
        vb_ref[...] = v.astype(BF16)
    kw = rope(seg(C_KW, IN_PAD), 3 * LANES)
    kw_ref[...] = kw
    kwb_ref[...] = kw.astype(BF16)
    ki_ref[...] = kw[:, :IDX_DIM]


def _inproj(x, g, w, tab, seq_len=None):
    n = x.shape[0]
    transposed = seq_len is not None
    tm = _tile(seq_len, 512) if transposed else _tile(n, 512)
    n_tab = tab.shape[0] // tm
    row = lambda width: pl.BlockSpec((tm, width), lambda i: (i, 0))
    rows = lambda width, dt: (row(width), jax.ShapeDtypeStruct((n, width), dt))
    if transposed:
        assert tm % LANES == 0
        nb, ns, qpt = n // seq_len, seq_len // tm, tm // LANES
        qt = (pl.BlockSpec((1, qpt, LANES, N_HEADS * LANES), lambda i: (i // ns, i % ns, 0, 0)),
              jax.ShapeDtypeStruct((nb, seq_len // LANES, LANES, N_HEADS * LANES), BF16))
        vt = (pl.BlockSpec((1, 1, LANES, tm), lambda i: (i // ns, i % ns, 0, 0)),
              jax.ShapeDtypeStruct((nb, ns, LANES, tm), BF16))
    else:
        qt = rows(N_HEADS * LANES, F32)
        vt = rows(LANES, BF16)
    outs = [rows(POOL_WIDTH, F32), qt, rows(LANES, F32), rows(LANES, F32), qt, rows(IDX_DIM, F32), rows(LANES, F32),
            rows(LANES, BF16), vt, rows(LANES, BF16)]
    return pl.pallas_call(
        functools.partial(_inproj_kernel, transposed=transposed),
        grid=(n // tm,),
        in_specs=[row(D_MODEL), pl.BlockSpec((1, D_MODEL), lambda i: (0, 0)),
                  pl.BlockSpec((D_MODEL, IN_PAD), lambda i: (0, 0)),
                  pl.BlockSpec((tm, 6 * LANES), lambda i: (i % n_tab, 0))],
        out_specs=[spec for spec, _ in outs],
        out_shape=[shape for _, shape in outs],
        compiler_params=_params(("arbitrary",)),
    )(x, g, w, tab)


def _pool_prompt_kernel(u_ref, pw_ref, sc_ref, o_ref, ext_ref):
    t = pl.program_id(1)
    tp = u_ref.shape[1]
    halo = POOL_STATE + 1

    @pl.when(t == 0)
    def _():
        ext_ref[0:halo, :] = jnp.zeros((halo, POOL_WIDTH), F32)

    @pl.when(t > 0)
    def _():
        ext_ref[0:halo, :] = ext_ref[tp:tp + halo, :]

    ext_ref[halo:halo + tp, :] = u_ref[0]
    pos = t * tp + lax.broadcasted_iota(I32, (tp, 1), 0)
    for g, w in enumerate(POOL_WINDOWS):
        c0 = g * POOL_GROUP
        cur = ext_ref[halo:halo + tp, c0:c0 + POOL_GROUP]
        win = cur
        for j in range(1, w):
            win = win + ext_ref[halo - j:halo - j + tp, c0:c0 + POOL_GROUP]
        cnt = jnp.minimum(pos + 1, w).astype(F32)
        d = (win / cnt - cur).astype(BF16)
        o = jnp.dot(d, pw_ref[g], preferred_element_type=F32) * sc_ref[:, c0:c0 + POOL_GROUP]
        o_ref[0, :, c0:c0 + POOL_GROUP] = o.astype(o_ref.dtype)


def _pool_prompt(u, pw, sc):
    b, s, _ = u.shape
    tp = _tile(s, 512)
    return pl.pallas_call(
        _pool_prompt_kernel,
        grid=(b, s // tp),
        in_specs=[pl.BlockSpec((1, tp, POOL_WIDTH), lambda i, t: (i, t, 0)),
                  pl.BlockSpec((len(POOL_WINDOWS), POOL_GROUP, POOL_GROUP), lambda i, t: (0, 0, 0)),
                  pl.BlockSpec((1, POOL_WIDTH), lambda i, t: (0, 0))],
        out_specs=pl.BlockSpec((1, tp, POOL_WIDTH), lambda i, t: (i, t, 0)),
        out_shape=jax.ShapeDtypeStruct((b, s, POOL_WIDTH), BF16),
        scratch_shapes=[pltpu.VMEM((tp + POOL_STATE + 1, POOL_WIDTH), F32)],
        compiler_params=_params(("arbitrary", "arbitrary")),
    )(u, pw, sc)


def _pool_sample_kernel(st_ref, u_ref, pw_ref, sc_ref, o_ref, *, pos0):
    t_new = u_ref.shape[0]
    ext = [st_ref[i] for i in range(POOL_STATE)] + [u_ref[i] for i in range(t_new)]
    for g, w in enumerate(POOL_WINDOWS):
        c0 = g * POOL_GROUP
        ds = []
        for t in range(t_new):
            e = POOL_STATE + t
            cur = ext[e][:, c0:c0 + POOL_GROUP]
            win = cur
            for j in range(1, w):
                win = win + ext[e - j][:, c0:c0 + POOL_GROUP]
            cnt = float(min(pos0 + t + 1, w))
            ds.append((win / cnt - cur).astype(BF16))
        d = jnp.concatenate(ds, axis=0)
        o = jnp.dot(d, pw_ref[g], preferred_element_type=F32) * sc_ref[:, c0:c0 + POOL_GROUP]
        db = o.shape[0] // t_new
        for t in range(t_new):
            o_ref[t, :, c0:c0 + POOL_GROUP] = o[t * db:(t + 1) * db]


def _pool_sample(st_t, u_t, pw, sc, pos0):
    t_new, db, _ = u_t.shape
    return pl.pallas_call(
        functools.partial(_pool_sample_kernel, pos0=pos0),
        out_shape=jax.ShapeDtypeStruct((t_new, db, POOL_WIDTH), F32),
        compiler_params=pltpu.CompilerParams(vmem_limit_bytes=VMEM_LIMIT),
    )(st_t, u_t, pw, sc)


def _search_threshold(count_ge, shape, topk):
    def bit_body(it, t):
        cand = t + lax.shift_left(jnp.int32(1), 31 - it)
        return jnp.where(count_ge(cand) >= topk, cand, t)
    return lax.fori_loop(0, 32, bit_body, jnp.full(shape, INT_MIN, I32))


def _search_tie_index(count_eq_below, need, shape, n_bits):
    def bit_body(it, j):
        cand = j + lax.shift_left(jnp.int32(1), n_bits - 1 - it)
        return jnp.where(count_eq_below(cand) < need, cand, j)
    return lax.fori_loop(0, n_bits, bit_body, jnp.zeros(shape, I32))


def _place_heads(o_ref_store, heads, lane):
    for p in range(N_HEADS // 2):
        parts = []
        for hh in range(2):
            h = 2 * p + hh
            g = h // (N_HEADS // N_KV_HEADS)
            parts.append(heads[h] if g == hh else pltpu.roll(heads[h], HEAD_DIM, 1))
        o_ref_store(p, jnp.where(lane < HEAD_DIM, parts[0], parts[1]))


def _dsa_prompt_kernel(qt_ref, qit_ref, kwq_ref, kwb_ref, kb_ref, vt_ref, o_ref, key_s, j_s, *, topk, ck, n_bits):
    i = pl.program_id(1)
    tq = LANES
    nc = (i * tq) // ck + 1
    qpos = i * tq + lax.broadcasted_iota(I32, (ck, tq), 1)
    row = lax.broadcasted_iota(I32, (ck, tq), 0)
    hpg = N_HEADS // N_KV_HEADS

    wit =jnp.transpose(kwq_ref[0])[IDX_DIM:IDX_DIM + IDX_HEADS, :]

    def score_body(c, carry):
        kc = kwb_ref[0, pl.ds(pl.multiple_of(c * ck, ck), ck), :]
        st = jnp.dot(kc, qit_ref[0, 0], preferred_element_type=F32)
        acc = jnp.zeros((ck, tq), F32)
        for h in range(IDX_HEADS):
            acc = acc + wit[h:h + 1, :] * jnp.maximum(st[:, h * tq:(h + 1) * tq], 0.0)
        score = jnp.where(c * ck + row <= qpos, acc + 0.0, -jnp.inf)
        key_s[c] = _sort_key(score)
        return carry

    lax.fori_loop(0, nc, score_body, 0)

    def count(pred):
        def body(c, acc):
            return acc + _fold_rows(jnp.where(pred(key_s[c], c * ck + row), 1.0, 0.0), jnp.add)
        acc = lax.fori_loop(0, nc, body, jnp.zeros((8, tq), F32))
        return jnp.sum(acc, axis=0, keepdims=True)

    thr = _search_threshold(lambda cand: count(lambda k, idx: k >= cand), (1, tq), float(topk))
    c_gt = count(lambda k, idx: k > thr)
    c_ge = count(lambda k, idx: k >= thr)
    need = float(topk) - c_gt
    j_s[...] = jnp.full((1, tq), INT_MAX, I32)

    @pl.when(jnp.max(c_ge - c_gt - need) > 0.0)
    def _():
        j_s[...] = _search_tie_index(lambda cand: count(lambda k, idx: (k == thr) & (idx < cand)),
                                     need, (1, tq), n_bits)

    jmax = j_s[...]

    gw = hpg * tq

    def att_body(c, carry):
        off = pl.multiple_of(c * ck, ck)
        k = key_s[c]
        idx = c * ck + row
        sel = (k > thr) | ((k == thr) & (idx <= jmax))
        b = jnp.where(sel & (idx <= qpos), 0.0, NEG)
        bias = jnp.concatenate([b] * hpg, axis=1)
        kc = kb_ref[0, pl.ds(off, ck), :]
        vc = vt_ref[0, c]
        out = []
        for g in range(N_KV_HEADS):
            m, l, acc = carry[g]
            st = jnp.dot(kc, qt_ref[0, 0, :, g * gw:(g + 1) * gw], preferred_element_type=F32) + bias
            m_new = jnp.maximum(m, jnp.max(_fold_rows(st, jnp.maximum), axis=0, keepdims=True))
            alpha = jnp.exp(m - m_new)
            p = jnp.exp(st - m_new)
            l = alpha * l + jnp.sum(_fold_rows(p, jnp.add), axis=0, keepdims=True)
            acc = alpha * acc + jnp.dot(vc, p.astype(BF16), preferred_element_type=F32)
            out.append((m_new, l, acc))
        return tuple(out)

    init = (jnp.full((1, gw), NEG, F32), jnp.zeros((1, gw), F32), jnp.zeros((LANES, gw), F32))
    res = lax.fori_loop(0, nc, att_body, (init,) * N_KV_HEADS)
    for g in range(N_KV_HEADS):
        _, l, acc = res[g]
        ot = (acc / l)[g * HEAD_DIM:(g + 1) * HEAD_DIM, :]
        for r in range(0, hpg, 2):
            pair = jnp.concatenate([ot[:, r * tq:(r + 1) * tq], ot[:, (r + 1) * tq:(r + 2) * tq]], axis=0)
            c0 = (g * hpg + r) * HEAD_DIM
            o_ref[0, :, c0:c0 + LANES] = jnp.transpose(pair).astype(o_ref.dtype)


def _dsa_prompt(qt, qit, kw, kwb, kb, vt, topk):
    b, s, _ = kb.shape
    tq = LANES
    ck = vt.shape[-1]
    assert ck >= topk and ck % tq == 0 and s % ck == 0
    n_bits = max(1, int(math.ceil(math.log2(s))) + 1)
    qspec = pl.BlockSpec((1, 1, LANES, N_HEADS * LANES), lambda bi, i: (bi, i, 0, 0))
    kspec = pl.BlockSpec((1, s, LANES), lambda bi, i: (bi, 0, 0))
    return pl.pallas_call(
        functools.partial(_dsa_prompt_kernel, topk=topk, ck=ck, n_bits=n_bits),
        grid=(b, s // tq),
        in_specs=[qspec, qspec, pl.BlockSpec((1, tq, LANES), lambda bi, i: (bi, i, 0)), kspec, kspec,
                  pl.BlockSpec((1, s // ck, LANES, ck), lambda bi, i: (bi, 0, 0, 0))],
        out_specs=pl.BlockSpec((1, tq, ATTN_WIDTH), lambda bi, i: (bi, i, 0)),
        out_shape=jax.ShapeDtypeStruct((b, s, ATTN_WIDTH), BF16),
        scratch_shapes=[pltpu.VMEM((s // ck, ck, tq), I32), pltpu.VMEM((1, tq), I32)],
        compiler_params=_params(("arbitrary", "arbitrary")),
    )(qt, qit, kw, kwb, kb, vt)


def _dsa_sample_kernel(pt_ref, q_ref, qi_ref, kw_ref, k_ref, v_ref, cik_hbm, ck_hbm, cv_hbm, o_ref,
                       ki_buf, k_buf, v_buf, sems, j_s, *, n_pages, topk, n_bits):
    b = pl.program_id(0)
    nb = pl.num_programs(0)
    slot = b % 2
    t_new = q_ref.shape[0]
    past = n_pages * PAGE_SIZE
    srcs = (cik_hbm, ck_hbm, cv_hbm)
    bufs = (ki_buf, k_buf, v_buf)

    def page_copy(a, page, sl, p):
        return pltpu.make_async_copy(srcs[a].at[page], bufs[a].at[sl, p], sems.at[a, sl])

    def start_all(bb, sl):
        def body(p, carry):
            page = pt_ref[bb * n_pages + p]
            for a in range(3):
                page_copy(a, page, sl, p).start()
            return carry
        lax.fori_loop(0, n_pages, body, 0)

    def wait_all(sl):
        def body(p, carry):
            for a in range(3):
                page_copy(a, 0, sl, p).wait()
            return carry
        lax.fori_loop(0, n_pages, body, 0)

    @pl.when(b == 0)
    def _():
        start_all(0, 0)

    @pl.when(b + 1 < nb)
    def _():
        start_all(b + 1, 1 - slot)

    wait_all(slot)

    pad_rows = LANES - t_new
    tq = lax.broadcasted_iota(I32, (t_new, 1), 0)
    lane_p = lax.broadcasted_iota(I32, (t_new, past), 1)
    lane_n = lax.broadcasted_iota(I32, (t_new, LANES), 1)
    rep = lambda x: jnp.concatenate([x] * N_HEADS, axis=0)

    qi = jnp.concatenate([qi_ref[:, h * LANES:h * LANES + IDX_DIM] for h in range(IDX_HEADS)], axis=0).astype(BF16)
    wi = kw_ref[:, IDX_DIM:IDX_DIM + IDX_HEADS]
    wcol = jnp.concatenate([wi[:, h:h + 1] for h in range(IDX_HEADS)], axis=0)
    ki_past = ki_buf[slot].reshape(past, IDX_DIM).astype(BF16)
    ki_new = jnp.concatenate([kw_ref[:, :IDX_DIM], jnp.zeros((pad_rows, IDX_DIM), F32)], axis=0).astype(BF16)

    def head_sum(r):
        acc = r[0:t_new]
        for h in range(1, IDX_HEADS):
            acc = acc + r[h * t_new:(h + 1) * t_new]
        return acc + 0.0

    score_p = head_sum(wcol * jnp.maximum(_dot_nt(qi, ki_past), 0.0))
    score_n = jnp.where(lane_n <= tq, head_sum(wcol * jnp.maximum(_dot_nt(qi, ki_new), 0.0)), -jnp.inf)
    key_p = _sort_key(score_p)
    key_n = _sort_key(score_n)
    idx_n = past + lane_n

    def count(pred):
        cp = jnp.sum(jnp.where(pred(key_p, lane_p), 1.0, 0.0), axis=1, keepdims=True)
        return cp + jnp.sum(jnp.where(pred(key_n, idx_n), 1.0, 0.0), axis=1, keepdims=True)

    thr = _search_threshold(lambda cand: count(lambda k, idx: k >= cand), (t_new, 1), float(topk))
    c_gt = count(lambda k, idx: k > thr)
    c_ge = count(lambda k, idx: k >= thr)
    need = float(topk) - c_gt
    j_s[...] = jnp.full((t_new, 1), INT_MAX, I32)

    @pl.when(jnp.max(c_ge - c_gt - need) > 0.0)
    def _():
        j_s[...] = _search_tie_index(lambda cand: count(lambda k, idx: (k == thr) & (idx < cand)),
                                     need, (t_new, 1), n_bits)

    jmax = j_s[...]
    sel = lambda k, idx: (k > thr) | ((k == thr) & (idx <= jmax))
    bias_p = jnp.where(sel(key_p, lane_p), 0.0, NEG)
    bias_n = jnp.where(sel(key_n, idx_n) & (lane_n <= tq), 0.0, NEG)

    q = jnp.concatenate([q_ref[:, h * LANES:(h + 1) * LANES] for h in range(N_HEADS)], axis=0).astype(BF16)
    k_past = k_buf[slot].reshape(past, LANES).astype(BF16)
    v_past = v_buf[slot].reshape(past, LANES).astype(BF16)
    zpad = jnp.zeros((pad_rows, LANES), F32)
    k_new = jnp.concatenate([k_ref[...], zpad], axis=0).astype(BF16)
    v_new = jnp.concatenate([v_ref[...], zpad], axis=0).astype(BF16)
    s_p = _dot_nt(q, k_past) + rep(bias_p)
    s_n = _dot_nt(q, k_new) + rep(bias_n)
    m = jnp.maximum(jnp.max(s_p, axis=1, keepdims=True), jnp.max(s_n, axis=1, keepdims=True))
    p_p = jnp.exp(s_p - m)
    p_n = jnp.exp(s_n - m)
    l = jnp.sum(p_p, axis=1, keepdims=True) + jnp.sum(p_n, axis=1, keepdims=True)
    o = (jnp.dot(p_p.astype(BF16), v_past, preferred_element_type=F32)
         + jnp.dot(p_n.astype(BF16), v_new, preferred_element_type=F32)) / l
    heads = [o[h * t_new:(h + 1) * t_new] for h in range(N_HEADS)]

    def store(p, val):
        o_ref[:, p * LANES:(p + 1) * LANES] = val

    _place_heads(store, heads, lane_n)


def _dsa_sample(page_table, q, qi, kw, k, v, cik, ck, cv, topk):
    db, n_pages = page_table.shape
    t_new = q.shape[0] // db
    past = n_pages * PAGE_SIZE
    n_bits = max(1, int(math.ceil(math.log2(past + LANES))) + 1)
    row = lambda width: pl.BlockSpec((t_new, width), lambda b, pt: (b, 0))
    any_spec = pl.BlockSpec(memory_space=pl.ANY)
    return pl.pallas_call(
        functools.partial(_dsa_sample_kernel, n_pages=n_pages, topk=topk, n_bits=n_bits),
        grid_spec=pltpu.PrefetchScalarGridSpec(
            num_scalar_prefetch=1,
            grid=(db,),
            in_specs=[row(N_HEADS * LANES), row(IDX_HEADS * LANES), row(LANES), row(LANES), row(LANES),
                      any_spec, any_spec, any_spec],
            out_specs=row(ATTN_WIDTH),
            scratch_shapes=[pltpu.VMEM((2, n_pages, PAGE_SIZE, IDX_DIM), F32),
                            pltpu.VMEM((2, n_pages, PAGE_SIZE, LANES), F32),
                            pltpu.VMEM((2, n_pages, PAGE_SIZE, LANES), F32),
                            pltpu.SemaphoreType.DMA((3, 2)),
                            pltpu.VMEM((t_new, 1), I32)]),
        out_shape=jax.ShapeDtypeStruct((db * t_new, ATTN_WIDTH), F32),
        compiler_params=pltpu.CompilerParams(dimension_semantics=("arbitrary",), vmem_limit_bytes=56 * 1024 * 1024),
    )(page_table.reshape(-1), q, qi, kw, k, v, cik, ck, cv)


def _outproj_router_kernel(pool_ref, attn_ref, x_ref, wo_ref, g_ref, wr_ref, br_ref, cin_ref,
                           h_ref, xe_ref, meta_ref, cnt_ref, carry_s):
    i = pl.program_id(0)
    tm = x_ref.shape[0]

    @pl.when(i == 0)
    def _():
        carry_s[...] = cin_ref[...]

    h1 = (x_ref[...]
          + jnp.dot(pool_ref[...].astype(BF16), wo_ref[0:POOL_WIDTH, :], preferred_element_type=F32)
          + jnp.dot(attn_ref[...].astype(BF16), wo_ref[POOL_WIDTH:, :], preferred_element_type=F32))
    h_ref[...] = h1
    xn = _rms(h1, g_ref[...])
    for j in range(ROW_TILES):
        xe_ref[pl.ds(j, tm, stride=XE_PITCH), :] = xn[:, j * LANES:(j + 1) * LANES]
    logits = jnp.dot(xn.astype(BF16), wr_ref[...], preferred_element_type=F32) + br_ref[...]

    lane = lax.broadcasted_iota(I32, (tm, LANES), 1).astype(F32)
    rmax = lambda v: jnp.max(v, axis=1, keepdims=True)
    rmin = lambda v: jnp.min(v, axis=1, keepdims=True)
    rsum = lambda v: jnp.sum(v, axis=1, keepdims=True)

    is_g = lane < N_GROUPS
    gl = jnp.where(is_g, logits, -jnp.inf)
    gmax = rmax(gl)
    g_sel = rmin(jnp.where(gl == gmax, lane, float(LANES)))
    g_w = 1.0 / rsum(jnp.where(is_g, jnp.exp(gl - gmax), 0.0))
    e0 = N_GROUPS + g_sel * EXPERTS_PER_GROUP
    is_e = (lane >= e0) & (lane < e0 + EXPERTS_PER_GROUP)
    el = jnp.where(is_e, logits, -jnp.inf)
    un = jnp.where(is_e, jnp.exp(el - rmax(el)), 0.0)
    prob = jnp.where(is_e, un / rsum(un), -1.0)
    p0 = rmax(prob)
    i0 = rmin(jnp.where(prob == p0, lane, float(LANES)))
    prob1 = jnp.where(lane == i0, -1.0, prob)
    p1 = rmax(prob1)
    i1 = rmin(jnp.where(prob1 == p1, lane, float(LANES)))
    w0 = g_w * p0 / (p0 + p1)
    w1 = g_w * p1 / (p0 + p1)
    lo = jnp.minimum(i0, i1) - e0
    hi = jnp.maximum(i0, i1) - e0
    cls = g_sel * PAIRS_PER_GROUP + lo * (2 * EXPERTS_PER_GROUP - 1 - lo) * 0.5 + hi - lo - 1.0
    w_a = jnp.where(i0 < i1, w0, w1)
    w_b = jnp.where(i0 < i1, w1, w0)
    xe_ref[pl.ds(ROW_TILES, tm, stride=XE_PITCH), :] = jnp.where(lane == 0.0, w_a, jnp.where(lane == 1.0, w_b, 0.0))

    onehot = lane == cls
    tri = (lax.broadcasted_iota(I32, (tm, tm), 1) < lax.broadcasted_iota(I32, (tm, tm), 0)).astype(BF16)
    before = jnp.dot(tri, onehot.astype(BF16), preferred_element_type=F32) + carry_s[...]
    rank = rsum(jnp.where(onehot, before, 0.0))
    carry_s[...] = carry_s[...] + jnp.sum(onehot.astype(F32), axis=0, keepdims=True)
    cnt_ref[...] = carry_s[...]
    meta = jnp.where(lane == 0.0, cls, jnp.where(lane == 1.0, rank, 0.0))
    meta_ref[...] = jnp.transpose(meta)[0:8, :].astype(I32)


def _outproj_router(pool_o, attn_o, x, wo, g, wr, br, counts_in):
    n = x.shape[0]
    tm = _tile(n, 512)
    row = lambda width: pl.BlockSpec((tm, width), lambda i: (i, 0))
    full = lambda shape: pl.BlockSpec(shape, lambda i: (0,) * len(shape))
    return pl.pallas_call(
        _outproj_router_kernel,
        grid=(n // tm,),
        in_specs=[row(POOL_WIDTH), row(ATTN_WIDTH), row(D_MODEL), full((D_MODEL, D_MODEL)), full((1, D_MODEL)),
                  full((D_MODEL, LANES)), full((1, LANES)), full((1, LANES))],
        out_specs=[row(D_MODEL), pl.BlockSpec((tm * XE_PITCH, LANES), lambda i: (i, 0)),
                   pl.BlockSpec((8, tm), lambda i: (0, i)), full((1, LANES))],
        out_shape=[jax.ShapeDtypeStruct((n, D_MODEL), F32), jax.ShapeDtypeStruct((n * XE_PITCH, LANES), F32),
                   jax.ShapeDtypeStruct((8, n), I32), jax.ShapeDtypeStruct((1, LANES), F32)],
        scratch_shapes=[pltpu.VMEM((1, LANES), F32)],
        compiler_params=_params(("arbitrary",)),
    )(pool_o, attn_o, x, wo, g, wr, br, counts_in)


def _load_dest(ps_ref, cls_hbm, rank_hbm, cls_s, rank_s, sem, base, tm):
    c1 = pltpu.make_async_copy(cls_hbm.at[pl.ds(base, tm)], cls_s, sem.at[0])
    c2 = pltpu.make_async_copy(rank_hbm.at[pl.ds(base, tm)], rank_s, sem.at[1])
    c1.start()
    c2.start()
    c1.wait()
    c2.wait()
    return lambda r: ps_ref[cls_s[r]] + rank_s[r]


def _row_dmas(tm, make_copy, dest):
    def start(r, carry):
        make_copy(r, dest(r)).start()
        return carry

    def wait(r, carry):
        make_copy(r, 0).wait()
        return carry

    lax.fori_loop(0, tm, start, 0, unroll=8)
    lax.fori_loop(0, tm, wait, 0, unroll=8)


def _dispatch_kernel(ps_ref, cls_hbm, rank_hbm, xe_ref, xs_in_hbm, xs_hbm, cls_s, rank_s, sem, *, tm):
    del xs_in_hbm
    base = pl.program_id(0) * tm
    dest = _load_dest(ps_ref, cls_hbm, rank_hbm, cls_s, rank_s, sem, base, tm)
    _row_dmas(tm, lambda r, d: pltpu.make_async_copy(xe_ref.at[pl.ds(r * XE_PITCH, XE_PITCH)],
                                                     xs_hbm.at[pl.ds(d * XE_PITCH, XE_PITCH)], sem.at[2]), dest)


def _dispatch(pstart, cls, rank, xe, xs):
    n = xe.shape[0] // XE_PITCH
    tm = _tile(n, 512)
    any_spec = pl.BlockSpec(memory_space=pl.ANY)
    return pl.pallas_call(
        functools.partial(_dispatch_kernel, tm=tm),
        grid_spec=pltpu.PrefetchScalarGridSpec(
            num_scalar_prefetch=1, grid=(n // tm,),
            in_specs=[any_spec, any_spec, pl.BlockSpec((tm * XE_PITCH, LANES), lambda i, ps: (i, 0)), any_spec],
            out_specs=any_spec,
            scratch_shapes=[pltpu.SMEM((tm,), I32), pltpu.SMEM((tm,), I32), pltpu.SemaphoreType.DMA((3,))]),
        out_shape=jax.ShapeDtypeStruct(xs.shape, xs.dtype),
        input_output_aliases={4: 0},
        compiler_params=pltpu.CompilerParams(dimension_semantics=("arbitrary",), has_side_effects=True,
                                             vmem_limit_bytes=VMEM_LIMIT),
    )(pstart, cls, rank, xe, xs)


def _ffn_kernel(ba_ref, bb_ref, nu_ref, xs_ref, wga_ref, wua_ref, wda_ref, wgb_ref, wub_ref, wdb_ref, ys_ref):
    j = pl.program_id(0)
    rows = MOE_ROWS

    @pl.when(j < nu_ref[0])
    def _():
        x = jnp.concatenate([xs_ref[pl.ds(t, rows, stride=XE_PITCH), :] for t in range(ROW_TILES)], axis=1).astype(BF16)
        wts = xs_ref[pl.ds(ROW_TILES, rows, stride=XE_PITCH), :]

        def expert(wg_ref, wu_ref, wd_ref):
            gate = jnp.dot(x, wg_ref[0], preferred_element_type=F32)
            up = jnp.dot(x, wu_ref[0], preferred_element_type=F32)
            hid = (gate / (1.0 + jnp.exp(-gate))) * up
            return jnp.dot(hid.astype(BF16), wd_ref[0], preferred_element_type=F32)

        y = wts[:, 0:1] * expert(wga_ref, wua_ref, wda_ref) + wts[:, 1:2] * expert(wgb_ref, wub_ref, wdb_ref)
        for t in range(ROW_TILES):
            ys_ref[pl.ds(t, rows, stride=ROW_TILES), :] = y[:, t * LANES:(t + 1) * LANES]

    @pl.when(j >= nu_ref[0])
    def _():
        ys_ref[...] = jnp.zeros(ys_ref.shape, F32)


def _ffn(blk_a, blk_b, n_used, xs, wg, wu, wd):
    n_blk = xs.shape[0] // (MOE_ROWS * XE_PITCH)
    up_a = pl.BlockSpec((1, D_MODEL, EXPERT_HIDDEN), lambda j, a, b, nu: (a[j], 0, 0))
    up_b = pl.BlockSpec((1, D_MODEL, EXPERT_HIDDEN), lambda j, a, b, nu: (b[j], 0, 0))
    dn_a = pl.BlockSpec((1, EXPERT_HIDDEN, D_MODEL), lambda j, a, b, nu: (a[j], 0, 0))
    dn_b = pl.BlockSpec((1, EXPERT_HIDDEN, D_MODEL), lambda j, a, b, nu: (b[j], 0, 0))
    return pl.pallas_call(
        _ffn_kernel,
        grid_spec=pltpu.PrefetchScalarGridSpec(
            num_scalar_prefetch=3, grid=(n_blk,),
            in_specs=[pl.BlockSpec((MOE_ROWS * XE_PITCH, LANES), lambda j, a, b, nu: (j, 0)),
                      up_a, up_a, dn_a, up_b, up_b, dn_b],
            out_specs=pl.BlockSpec((MOE_ROWS * ROW_TILES, LANES), lambda j, a, b, nu: (j, 0))),
        out_shape=jax.ShapeDtypeStruct((n_blk * MOE_ROWS * ROW_TILES, LANES), F32),
        compiler_params=_params(("arbitrary",)),
    )(blk_a, blk_b, n_used, xs, wg, wu, wd, wg, wu, wd)


def _combine_kernel(ps_ref, cls_hbm, rank_hbm, ys_hbm, h_ref, g_ref, y_ref, cls_s, rank_s, ybuf, sem, *, tm):
    base = pl.program_id(0) * tm
    dest = _load_dest(ps_ref, cls_hbm, rank_hbm, cls_s, rank_s, sem, base, tm)
    _row_dmas(tm, lambda r, d: pltpu.make_async_copy(ys_hbm.at[pl.ds(d * ROW_TILES, ROW_TILES)],
                                                     ybuf.at[pl.ds(r * ROW_TILES, ROW_TILES)], sem.at[2]), dest)
    y = jnp.concatenate([ybuf[pl.ds(t, tm, stride=ROW_TILES), :] for t in range(ROW_TILES)], axis=1)
    y_ref[...] = _rms(h_ref[...] + y, g_ref[...])


def _combine(pstart, cls, rank, ys, h1, g):
    n = h1.shape[0]
    tm = _tile(n, 512)
    any_spec = pl.BlockSpec(memory_space=pl.ANY)
    return pl.pallas_call(
        functools.partial(_combine_kernel, tm=tm),
        grid_spec=pltpu.PrefetchScalarGridSpec(
            num_scalar_prefetch=1, grid=(n // tm,),
            in_specs=[any_spec, any_spec, any_spec, pl.BlockSpec((tm, D_MODEL), lambda i, ps: (i, 0)),
                      pl.BlockSpec((1, D_MODEL), lambda i, ps: (0, 0))],
            out_specs=pl.BlockSpec((tm, D_MODEL), lambda i, ps: (i, 0)),
            scratch_shapes=[pltpu.SMEM((tm,), I32), pltpu.SMEM((tm,), I32), pltpu.VMEM((tm * ROW_TILES, LANES), F32),
                            pltpu.SemaphoreType.DMA((3,))]),
        out_shape=jax.ShapeDtypeStruct((n, D_MODEL), F32),
        compiler_params=_params(("arbitrary",)),
    )(pstart, cls, rank, ys, h1, g)


def _moe_and_final(pool_o, attn_o, x, wo, g_ffn, wr, br, wg, wu, wd, g_final):
    n = x.shape[0]
    h1, xe, meta, counts = _outproj_router(pool_o, attn_o, x, wo, g_ffn, wr, br, jnp.zeros((1, LANES), F32))
    cnt = counts[0].astype(I32)
    padded = ((cnt + MOE_ROWS - 1) // MOE_ROWS) * MOE_ROWS
    pends = jnp.cumsum(padded)
    pstart = (pends - padded).astype(I32)
    n_blk = -(-n // MOE_ROWS) + N_CLASSES
    n_used = (pends[-1] // MOE_ROWS).astype(I32)
    blk = jnp.arange(n_blk, dtype=I32)
    blk_last = jnp.minimum(blk, jnp.maximum(n_used - 1, 0))
    blk_cls = jnp.sum((pends[None, :N_CLASSES] <= (blk_last * MOE_ROWS)[:, None]).astype(I32), axis=1)
    blk_cls = jnp.minimum(blk_cls, N_CLASSES - 1)
    blk_a = jnp.asarray(CLASS_A)[blk_cls]
    blk_b = jnp.asarray(CLASS_B)[blk_cls]
    cls, rank = meta[0], meta[1]
    xs = _dispatch(pstart, cls, rank, xe, jnp.zeros((n_blk * MOE_ROWS * XE_PITCH, LANES), F32))
    ys = _ffn(blk_a, blk_b, n_used.reshape(1), xs, wg, wu, wd)
    return _combine(pstart, cls, rank, ys, h1, g_final)


def kernel(x_prompt, x_sample, cache_k, cache_v, cache_idx_k, state_pool, page_table, norm_mix_g, w_in, pool_w,
           pool_scale, w_out, norm_ffn_g, router_group_w, router_group_b, router_expert_w, router_expert_b,
           expert_w_gate, expert_w_up, expert_w_down, final_norm_g):
    depth = w_in.shape[0]
    assert depth == 1, "single-layer step"
    b, s, d = x_prompt.shape
    db, t_new, _ = x_sample.shape
    n_pages = page_table.shape[1]
    past = n_pages * PAGE_SIZE
    n_pool = cache_k.shape[1]
    l = 0

    w_in_b = jnp.pad(w_in[l], ((0, 0), (0, IN_PAD - IN_WIDTH))).astype(BF16)
    g_mix = norm_mix_g[l].reshape(1, d)
    pw = pool_w[l].astype(BF16)
    psc = pool_scale[l].reshape(1, POOL_WIDTH)
    wo = w_out[l].astype(BF16)
    g_ffn = norm_ffn_g[l].reshape(1, d)
    wr = jnp.pad(jnp.concatenate([router_group_w[l], router_expert_w[l]], axis=1),
                 ((0, 0), (0, LANES - N_GROUPS - N_EXPERTS))).astype(BF16)
    br = jnp.pad(jnp.concatenate([router_group_b[l], router_expert_b[l]]), (0, LANES - N_GROUPS - N_EXPERTS)).reshape(1, LANES)
    wg = expert_w_gate[l].astype(BF16)
    wu = expert_w_up[l].astype(BF16)
    wd = expert_w_down[l].astype(BF16)
    g_fin = final_norm_g.reshape(1, d)
    moe = lambda pool_o, attn_o, x: _moe_and_final(pool_o, attn_o, x, wo, g_ffn, wr, br, wg, wu, wd, g_fin)

    xp = x_prompt.reshape(b * s, d)
    tab_p = _rope_tables(jnp.arange(s, dtype=I32))
    u, qt, k, v, qit, ki, kw, kb, vt, kwb = _inproj(xp, g_mix, w_in_b, tab_p, seq_len=s)
    r3 = lambda a: a.reshape(b, s, a.shape[-1])
    pool_o = _pool_prompt(r3(u), pw, psc)
    attn_o = _dsa_prompt(qt, qit, r3(kw), r3(kwb), r3(kb), vt, min(TOPK_MAX, s // 4))
    y_prompt = moe(pool_o.reshape(b * s, POOL_WIDTH), attn_o.reshape(b * s, ATTN_WIDTH), xp).reshape(b, s, d)
    k_prompt = k.reshape(1, b, s, N_KV_HEADS, HEAD_DIM)
    v_prompt = v.reshape(1, b, s, N_KV_HEADS, HEAD_DIM)
    idx_k_prompt = ki.reshape(1, b, s, IDX_DIM)
    if s >= POOL_STATE:
        pool_prompt = r3(u)[:, s - POOL_STATE:][None]
    else:
        pool_prompt = jnp.concatenate([jnp.zeros((b, POOL_STATE, POOL_WIDTH), F32), r3(u)], axis=1)[:, -POOL_STATE:][None]

    xs_ = x_sample.reshape(db * t_new, d)
    tab_s = _rope_tables(past + (jnp.arange(db * t_new, dtype=I32) % t_new))
    u2, q2, k2, v2, qi2, ki2, kw2, _, _, _ = _inproj(xs_, g_mix, w_in_b, tab_s)
    u2_3 = u2.reshape(db, t_new, POOL_WIDTH)
    pool2 = _pool_sample(jnp.swapaxes(state_pool[l], 0, 1), jnp.swapaxes(u2_3, 0, 1), pw, psc, past)
    pool2 = jnp.swapaxes(pool2, 0, 1).reshape(db * t_new, POOL_WIDTH)
    attn2 = _dsa_sample(page_table, q2, qi2, kw2, k2, v2,
                        cache_idx_k[l], cache_k[l].reshape(n_pool, PAGE_SIZE, LANES),
                        cache_v[l].reshape(n_pool, PAGE_SIZE, LANES), min(TOPK_MAX, (past + t_new) // 4))
    y_sample = moe(pool2, attn2, xs_).reshape(db, t_new, d)
    k_sample = k2.reshape(1, db, t_new, N_KV_HEADS, HEAD_DIM)
    v_sample = v2.reshape(1, db, t_new, N_KV_HEADS, HEAD_DIM)
    idx_k_sample = ki2.reshape(1, db, t_new, IDX_DIM)
    pool_sample = jnp.concatenate([state_pool[l], u2_3], axis=1)[:, -POOL_STATE:][None]

    return (y_prompt, y_sample, k_prompt, v_prompt, idx_k_prompt, pool_prompt,
            k_sample, v_sample, idx_k_sample, pool_sample)
```

```python
import functools
import math

import numpy as np
import jax
import jax.numpy as jnp
from jax import lax
from jax.experimental import pallas as pl
from jax.experimental.pallas import tpu as pltpu

F32 = jnp.float32
BF16 = jnp.bfloat16
I32 = jnp.int32

D_MODEL = 1024
HEAD_DIM = 64
N_HEADS = 8
N_KV_HEADS = 2
ROPE_DIM = 16
ROPE_THETA = 500000.0
IDX_HEADS = 8
IDX_DIM = 64
TOPK_MAX = 256
PAGE_SIZE = 128
POOL_WIDTH = 512
POOL_WINDOWS = (2, 4, 8, 16)
POOL_GROUP = 128
POOL_STATE = 15
ATTN_WIDTH = 512
N_GROUPS = 4
EXPERTS_PER_GROUP = 8
N_EXPERTS = 32
EXPERT_HIDDEN = 256
NORM_EPS = 1e-6

LANES = 128
IN_WIDTH = 1864
IN_PAD = 1920
C_U, C_Q, C_K, C_V, C_QI, C_KW = 0, 512, 1024, 1152, 1280, 1792
PAIRS_PER_GROUP = EXPERTS_PER_GROUP * (EXPERTS_PER_GROUP - 1) // 2
N_CLASSES = N_GROUPS * PAIRS_PER_GROUP
MOE_ROWS = 128
ROW_TILES = D_MODEL // LANES
XE_PITCH = ROW_TILES + 1
ATT_SUB_HEADS = 4
NEG = -1e30
INT_MIN = -2 ** 31
INT_MAX = 2 ** 31 - 1
VMEM_LIMIT = 48 * 1024 * 1024


def _class_tables():
    a, b = [], []
    for g in range(N_GROUPS):
        for lo in range(EXPERTS_PER_GROUP):
            for hi in range(lo + 1, EXPERTS_PER_GROUP):
                a.append(g * EXPERTS_PER_GROUP + lo)
                b.append(g * EXPERTS_PER_GROUP + hi)
    return np.array(a, np.int32), np.array(b, np.int32)


CLASS_A, CLASS_B = _class_tables()


def _tile(n, cap):
    t = min(n, cap)
    while n % t:
        t -= 8
    return t


def _params(sem):
    return pltpu.CompilerParams(dimension_semantics=sem, vmem_limit_bytes=VMEM_LIMIT)


def _rms(x, g):
    return x * lax.rsqrt(jnp.mean(x * x, axis=-1, keepdims=True) + NORM_EPS) * g


def _dot_nt(a, b):
    return lax.dot_general(a, b, (((1,), (1,)), ((), ())), preferred_element_type=F32)


def _tree(parts, op):
    parts = list(parts)
    while len(parts) > 1:
        nxt = [op(parts[a], parts[a + 1]) for a in range(0, len(parts) - 1, 2)]
        if len(parts) % 2:
            nxt.append(parts[-1])
        parts = nxt
    return parts[0]


def _fold_rows(x, op):
    return _tree([x[r:r + 8] for r in range(0, x.shape[0], 8)], op)


def _count_rows(mask, n_acc=8):
    accs = [jnp.zeros((8, mask.shape[1]), F32) for _ in range(n_acc)]
    for a, r in enumerate(range(0, mask.shape[0], 8)):
        acc = accs[a % n_acc]
        accs[a % n_acc] = jnp.where(mask[r:r + 8], acc + 1.0, acc)
    return _fold_rows(jnp.concatenate(accs, axis=0), jnp.add)


def _sort_key(score):
    bits = lax.bitcast_convert_type(score, I32)
    return jnp.where(bits < 0, bits ^ jnp.int32(INT_MAX), bits)


def _rope_tables(pos):
    half = ROPE_DIM // 2
    r = pos.shape[0]
    inv = jnp.power(jnp.float32(ROPE_THETA), -jnp.arange(half, dtype=F32) / half)
    ang = pos.astype(F32)[:, None] * inv[None, :]
    cos, sin = jnp.cos(ang), jnp.sin(ang)
    one = lambda n: jnp.ones((r, n), F32)
    zero = lambda n: jnp.zeros((r, n), F32)
    c64 = jnp.concatenate([cos, cos, one(HEAD_DIM - ROPE_DIM)], axis=1)
    s1_64 = jnp.concatenate([-sin, zero(HEAD_DIM - half)], axis=1)
    s2_64 = jnp.concatenate([zero(half), sin, zero(HEAD_DIM - ROPE_DIM)], axis=1)
    ck = jnp.concatenate([c64, jnp.full((r, IDX_HEADS), IDX_HEADS ** -0.5, F32), one(LANES - HEAD_DIM - IDX_HEADS)], axis=1)
    return jnp.concatenate([c64, c64, s1_64, s1_64, s2_64, s2_64, ck, s1_64, zero(64), s2_64, zero(64)], axis=1)


def _inproj_kernel(x_ref, g_ref, w_ref, tab_ref, u_ref, q_ref, k_ref, v_ref, qi_ref, ki_ref, kw_ref,
                   kb_ref, vb_ref, kwb_ref, *, transposed):
    xb = _rms(x_ref[...], g_ref[...]).astype(BF16)
    tm = xb.shape[0]
    lane = lax.broadcasted_iota(I32, (tm, LANES), 1)

    def seg(c0, c1):
        return jnp.dot(xb, w_ref[:, c0:c1], preferred_element_type=F32)

    def rope(z, t0):
        c, s1, s2 = (tab_ref[:, t0 + i * LANES:t0 + (i + 1) * LANES] for i in range(3))
        return z * c + pltpu.roll(z, LANES - ROPE_DIM // 2, 1) * s1 + pltpu.roll(z, ROPE_DIM // 2, 1) * s2

    u_ref[...] = seg(C_U, C_Q)

    for c_base, out_ref, to_group in ((C_Q, q_ref, True), (C_QI, qi_ref, False)):
        z = seg(c_base, c_base + N_HEADS * HEAD_DIM)
        for p in range(N_HEADS // 2):
            zz = rope(z[:, p * LANES:(p + 1) * LANES], 0) * (HEAD_DIM ** -0.5)
            zr = pltpu.roll(zz, HEAD_DIM, 1)
            for hh in range(2):
                h = 2 * p + hh
                tgt = (h // (N_HEADS // N_KV_HEADS)) if to_group else 0
                src = zz if hh == tgt else zr
                keep = (lane >= tgt * HEAD_DIM) & (lane < (tgt + 1) * HEAD_DIM)
                val = jnp.where(keep, src, 0.0)
                if transposed:
                    for qb in range(tm // LANES):
                        blk = jnp.transpose(val[qb * LANES:(qb + 1) * LANES])
                        out_ref[0, qb, :, h * LANES:(h + 1) * LANES] = blk.astype(out_ref.dtype)
                else:
                    out_ref[:, h * LANES:(h + 1) * LANES] = val.astype(out_ref.dtype)

    k = rope(seg(C_K, C_V), 0)
    kb_ref[...] = k.astype(BF16)
    v = seg(C_V, C_QI)
    kw = rope(seg(C_KW, IN_PAD), 3 * LANES)
    kw_ref[...] = kw
    kwb_ref[...] = kw.astype(BF16)
    if transposed:
        for qb in range(tm // LANES):
            cols = slice(qb * LANES, (qb + 1) * LANES)
            k_ref[0, :, cols] = jnp.transpose(k[cols])
            vt = jnp.transpose(v[cols])
            v_ref[0, :, cols] = vt
            vb_ref[0, 0, :, cols] = vt.astype(BF16)
            ki_ref[0, :, cols] = jnp.transpose(kw[cols])[:IDX_DIM]
    else:
        k_ref[...] = k
        v_ref[...] = v
        vb_ref[...] = v.astype(BF16)
        ki_ref[...] = kw[:, :IDX_DIM]


def _inproj(x, g, w, tab, seq_len=None):
    n = x.shape[0]
    transposed = seq_len is not None
    tm = _tile(seq_len, 512) if transposed else _tile(n, 512)
    n_tab = tab.shape[0] // tm
    row = lambda width: pl.BlockSpec((tm, width), lambda i: (i, 0))
    rows = lambda width, dt: (row(width), jax.ShapeDtypeStruct((n, width), dt))
    if transposed:
        assert tm % LANES == 0
        nb, ns, qpt = n // seq_len, seq_len // tm, tm // LANES
        qt = (pl.BlockSpec((1, qpt, LANES, N_HEADS * LANES), lambda i: (i // ns, i % ns, 0, 0)),
              jax.ShapeDtypeStruct((nb, seq_len // LANES, LANES, N_HEADS * LANES), BF16))
        vt = (pl.BlockSpec((1, 1, LANES, tm), lambda i: (i // ns, i % ns, 0, 0)),
              jax.ShapeDtypeStruct((nb, ns, LANES, tm), BF16))
        tr = lambda width: (pl.BlockSpec((1, width, tm), lambda i: (i // ns, 0, i % ns)),
                            jax.ShapeDtypeStruct((nb, width, seq_len), F32))
        kf, vf, kif = tr(LANES), tr(LANES), tr(IDX_DIM)
    else:
        qt = rows(N_HEADS * LANES, F32)
        vt = rows(LANES, BF16)
        kf, vf, kif = rows(LANES, F32), rows(LANES, F32), rows(IDX_DIM, F32)
    outs = [rows(POOL_WIDTH, F32), qt, kf, vf, qt, kif, rows(LANES, F32), rows(LANES, BF16), vt, rows(LANES, BF16)]
    return pl.pallas_call(
        functools.partial(_inproj_kernel, transposed=transposed),
        grid=(n // tm,),
        in_specs=[row(D_MODEL), pl.BlockSpec((1, D_MODEL), lambda i: (0, 0)),
                  pl.BlockSpec((D_MODEL, IN_PAD), lambda i: (0, 0)),
                  pl.BlockSpec((tm, 6 * LANES), lambda i: (i % n_tab, 0))],
        out_specs=[spec for spec, _ in outs],
        out_shape=[shape for _, shape in outs],
        compiler_params=_params(("arbitrary",)),
    )(x, g, w, tab)


def _pool_prompt_kernel(u_ref, pw_ref, sc_ref, o_ref, ext_ref):
    t = pl.program_id(1)
    tp = u_ref.shape[1]
    halo = POOL_STATE + 1

    @pl.when(t == 0)
    def _():
        ext_ref[0:halo, :] = jnp.zeros((halo, POOL_WIDTH), F32)

    @pl.when(t > 0)
    def _():
        ext_ref[0:halo, :] = ext_ref[tp:tp + halo, :]

    ext_ref[halo:halo + tp, :] = u_ref[0]
    pos = t * tp + lax.broadcasted_iota(I32, (tp, 1), 0)
    for g, w in enumerate(POOL_WINDOWS):
        c0 = g * POOL_GROUP
        cur = ext_ref[halo:halo + tp, c0:c0 + POOL_GROUP]
        win = cur
        for j in range(1, w):
            win = win + ext_ref[halo - j:halo - j + tp, c0:c0 + POOL_GROUP]
        cnt = jnp.minimum(pos + 1, w).astype(F32)
        d = (win / cnt - cur).astype(BF16)
        o = jnp.dot(d, pw_ref[g], preferred_element_type=F32) * sc_ref[:, c0:c0 + POOL_GROUP]
        o_ref[0, :, c0:c0 + POOL_GROUP] = o.astype(o_ref.dtype)


def _pool_prompt(u, pw, sc):
    b, s, _ = u.shape
    tp = _tile(s, 512)
    return pl.pallas_call(
        _pool_prompt_kernel,
        grid=(b, s // tp),
        in_specs=[pl.BlockSpec((1, tp, POOL_WIDTH), lambda i, t: (i, t, 0)),
                  pl.BlockSpec((len(POOL_WINDOWS), POOL_GROUP, POOL_GROUP), lambda i, t: (0, 0, 0)),
                  pl.BlockSpec((1, POOL_WIDTH), lambda i, t: (0, 0))],
        out_specs=pl.BlockSpec((1, tp, POOL_WIDTH), lambda i, t: (i, t, 0)),
        out_shape=jax.ShapeDtypeStruct((b, s, POOL_WIDTH), BF16),
        scratch_shapes=[pltpu.VMEM((tp + POOL_STATE + 1, POOL_WIDTH), F32)],
        compiler_params=_params(("arbitrary", "arbitrary")),
    )(u, pw, sc)


def _pool_sample_kernel(st_ref, u_ref, pw_ref, sc_ref, o_ref, *, pos0):
    t_new = u_ref.shape[0]
    ext = [st_ref[i] for i in range(POOL_STATE)] + [u_ref[i] for i in range(t_new)]
    for g, w in enumerate(POOL_WINDOWS):
        c0 = g * POOL_GROUP
        ds = []
        for t in range(t_new):
            e = POOL_STATE + t
            cur = ext[e][:, c0:c0 + POOL_GROUP]
            win = cur
            for j in range(1, w):
                win = win + ext[e - j][:, c0:c0 + POOL_GROUP]
            cnt = float(min(pos0 + t + 1, w))
            ds.append((win / cnt - cur).astype(BF16))
        d = jnp.concatenate(ds, axis=0)
        o = jnp.dot(d, pw_ref[g], preferred_element_type=F32) * sc_ref[:, c0:c0 + POOL_GROUP]
        db = o.shape[0] // t_new
        for t in range(t_new):
            o_ref[t, :, c0:c0 + POOL_GROUP] = o[t * db:(t + 1) * db]


def _pool_sample(st_t, u_t, pw, sc, pos0):
    t_new, db, _ = u_t.shape
    return pl.pallas_call(
        functools.partial(_pool_sample_kernel, pos0=pos0),
        out_shape=jax.ShapeDtypeStruct((t_new, db, POOL_WIDTH), F32),
        compiler_params=pltpu.CompilerParams(vmem_limit_bytes=VMEM_LIMIT),
    )(st_t, u_t, pw, sc)


def _search_threshold(count_ge, shape, topk):
    def bit_body(it, t):
        cand = t + lax.shift_left(jnp.int32(1), 31 - it)
        return jnp.where(count_ge(cand) >= topk, cand, t)
    return lax.fori_loop(0, 32, bit_body, jnp.full(shape, INT_MIN, I32))


def _search_tie_index(count_eq_below, need, shape, n_bits):
    def bit_body(it, j):
        cand = j + lax.shift_left(jnp.int32(1), n_bits - 1 - it)
        return jnp.where(count_eq_below(cand) < need, cand, j)
    return lax.fori_loop(0, n_bits, bit_body, jnp.zeros(shape, I32))


def _place_heads(o_ref_store, heads, lane):
    for p in range(N_HEADS // 2):
        parts = []
        for hh in range(2):
            h = 2 * p + hh
            g = h // (N_HEADS // N_KV_HEADS)
            parts.append(heads[h] if g == hh else pltpu.roll(heads[h], HEAD_DIM, 1))
        o_ref_store(p, jnp.where(lane < HEAD_DIM, parts[0], parts[1]))


def _dsa_prompt_kernel(qt_ref, qit_ref, kwq_ref, kwb_ref, kb_ref, vt_ref, o_ref, key_s, j_s, *, topk, ck, n_bits):
    i = pl.program_id(1)
    tq = LANES
    nc = (i * tq) // ck + 1
    qpos = i * tq + lax.broadcasted_iota(I32, (ck, tq), 1)
    row = lax.broadcasted_iota(I32, (ck, tq), 0)
    hpg = N_HEADS // N_KV_HEADS

    wit =jnp.transpose(kwq_ref[0])[IDX_DIM:IDX_DIM + IDX_HEADS, :]

    def score_body(c, carry):
        kc = kwb_ref[0, pl.ds(pl.multiple_of(c * ck, ck), ck), :]
        st = jnp.dot(kc, qit_ref[0, 0], preferred_element_type=F32)
        acc = jnp.zeros((ck, tq), F32)
        for h in range(IDX_HEADS):
            acc = acc + wit[h:h + 1, :] * jnp.maximum(st[:, h * tq:(h + 1) * tq], 0.0)
        score = jnp.where(c * ck + row <= qpos, acc + 0.0, -jnp.inf)
        key_s[c] = _sort_key(score)
        return carry

    lax.fori_loop(0, nc, score_body, 0)

    def count(pred):
        def body(c, acc):
            return acc + _count_rows(pred(key_s[c], c * ck + row))
        acc = lax.fori_loop(0, nc, body, jnp.zeros((8, tq), F32))
        return jnp.sum(acc, axis=0, keepdims=True)

    thr = _search_threshold(lambda cand: count(lambda k, idx: k >= cand), (1, tq), float(topk))
    c_gt = count(lambda k, idx: k > thr)
    c_ge = count(lambda k, idx: k >= thr)
    need = float(topk) - c_gt
    j_s[...] = jnp.full((1, tq), INT_MAX, I32)

    @pl.when(jnp.max(c_ge - c_gt - need) > 0.0)
    def _():
        j_s[...] = _search_tie_index(lambda cand: count(lambda k, idx: (k == thr) & (idx < cand)),
                                     need, (1, tq), n_bits)

    jmax = j_s[...]

    hps = ATT_SUB_HEADS
    sw = hps * tq
    n_sub = N_HEADS // hps

    def att_body(c, carry):
        off = pl.multiple_of(c * ck, ck)
        k = key_s[c]
        idx = c * ck + row
        sel = (k > thr) | ((k == thr) & (idx <= jmax))
        b = jnp.where(sel & (idx <= qpos), 0.0, NEG)
        bias = jnp.concatenate([b] * hps, axis=1)
        kc = kb_ref[0, pl.ds(off, ck), :]
        vc = vt_ref[0, c]
        out = []
        for sb in range(n_sub):
            m, l, acc = carry[sb]
            st = jnp.dot(kc, qt_ref[0, 0, :, sb * sw:(sb + 1) * sw], preferred_element_type=F32) + bias
            m_new = jnp.maximum(m, jnp.max(_fold_rows(st, jnp.maximum), axis=0, keepdims=True))
            alpha = jnp.exp(m - m_new)
            p = jnp.exp(st - m_new)
            l = alpha * l + jnp.sum(_fold_rows(p, jnp.add), axis=0, keepdims=True)
            acc = alpha * acc + jnp.dot(vc, p.astype(BF16), preferred_element_type=F32)
            out.append((m_new, l, acc))
        return tuple(out)

    init = (jnp.full((1, sw), NEG, F32), jnp.zeros((1, sw), F32), jnp.zeros((LANES, sw), F32))
    res = lax.fori_loop(0, nc, att_body, (init,) * n_sub)
    for sb in range(n_sub):
        _, l, acc = res[sb]
        g = (sb * hps) // hpg
        ot = (acc / l)[g * HEAD_DIM:(g + 1) * HEAD_DIM, :]
        for r in range(0, hps, 2):
            pair = jnp.concatenate([ot[:, r * tq:(r + 1) * tq], ot[:, (r + 1) * tq:(r + 2) * tq]], axis=0)
            c0 = (sb * hps + r) * HEAD_DIM
            o_ref[0, :, c0:c0 + LANES] = jnp.transpose(pair).astype(o_ref.dtype)


def _dsa_prompt(qt, qit, kw, kwb, kb, vt, topk):
    b, s, _ = kb.shape
    tq = LANES
    ck = vt.shape[-1]
    assert ck >= topk and ck % tq == 0 and s % ck == 0
    n_bits = max(1, int(math.ceil(math.log2(s))) + 1)
    qspec = pl.BlockSpec((1, 1, LANES, N_HEADS * LANES), lambda bi, i: (bi, i, 0, 0))
    kspec = pl.BlockSpec((1, s, LANES), lambda bi, i: (bi, 0, 0))
    return pl.pallas_call(
        functools.partial(_dsa_prompt_kernel, topk=topk, ck=ck, n_bits=n_bits),
        grid=(b, s // tq),
        in_specs=[qspec, qspec, pl.BlockSpec((1, tq, LANES), lambda bi, i: (bi, i, 0)), kspec, kspec,
                  pl.BlockSpec((1, s // ck, LANES, ck), lambda bi, i: (bi, 0, 0, 0))],
        out_specs=pl.BlockSpec((1, tq, ATTN_WIDTH), lambda bi, i: (bi, i, 0)),
        out_shape=jax.ShapeDtypeStruct((b, s, ATTN_WIDTH), BF16),
        scratch_shapes=[pltpu.VMEM((s // ck, ck, tq), I32), pltpu.VMEM((1, tq), I32)],
        compiler_params=_params(("arbitrary", "arbitrary")),
    )(qt, qit, kw, kwb, kb, vt)


def _dsa_sample_kernel(pt_ref, q_ref, qi_ref, kw_ref, k_ref, v_ref, cik_hbm, ck_hbm, cv_hbm, o_ref,
                       ki_buf, k_buf, v_buf, sems, j_s, *, n_pages, topk, n_bits):
    b = pl.program_id(0)
    nb = pl.num_programs(0)
    slot = b % 2
    t_new = q_ref.shape[0]
    past = n_pages * PAGE_SIZE
    srcs = (cik_hbm, ck_hbm, cv_hbm)
    bufs = (ki_buf, k_buf, v_buf)

    def page_copy(a, page, sl, p):
        return pltpu.make_async_copy(srcs[a].at[page], bufs[a].at[sl, p], sems.at[a, sl])

    def start_all(bb, sl):
        def body(p, carry):
            page = pt_ref[bb * n_pages + p]
            for a in range(3):
                page_copy(a, page, sl, p).start()
            return carry
        lax.fori_loop(0, n_pages, body, 0)

    def wait_all(sl):
        def body(p, carry):
            for a in range(3):
                page_copy(a, 0, sl, p).wait()
            return carry
        lax.fori_loop(0, n_pages, body, 0)

    @pl.when(b == 0)
    def _():
        start_all(0, 0)

    @pl.when(b + 1 < nb)
    def _():
        start_all(b + 1, 1 - slot)

    wait_all(slot)

    pad_rows = LANES - t_new
    tq = lax.broadcasted_iota(I32, (t_new, 1), 0)
    lane_n = lax.broadcasted_iota(I32, (t_new, LANES), 1)
    idx_p = (lax.broadcasted_iota(I32, (n_pages, t_new, PAGE_SIZE), 0) * PAGE_SIZE
             + lax.broadcasted_iota(I32, (n_pages, t_new, PAGE_SIZE), 2))
    rep = lambda x: jnp.concatenate([x] * N_HEADS, axis=x.ndim - 2)
    paged = lambda a: jnp.broadcast_to(a[None], (n_pages,) + a.shape)
    bmm = lambda a, b_, nt: lax.dot_general(a, b_, (((2,), (2 if nt else 1,)), ((0,), (0,))),
                                            preferred_element_type=F32)
    over_pages = lambda x, op: _tree([x[p] for p in range(n_pages)], op)

    qi = jnp.concatenate([qi_ref[:, h * LANES:h * LANES + IDX_DIM] for h in range(IDX_HEADS)], axis=0).astype(BF16)
    wi = kw_ref[:, IDX_DIM:IDX_DIM + IDX_HEADS]
    wcol = jnp.concatenate([wi[:, h:h + 1] for h in range(IDX_HEADS)], axis=0)
    ki_new = jnp.concatenate([kw_ref[:, :IDX_DIM], jnp.zeros((pad_rows, IDX_DIM), F32)], axis=0).astype(BF16)

    def head_sum(r):
        acc = r[..., 0:t_new, :]
        for h in range(1, IDX_HEADS):
            acc = acc + r[..., h * t_new:(h + 1) * t_new, :]
        return acc + 0.0

    score_p = head_sum(wcol * jnp.maximum(bmm(paged(qi), ki_buf[slot].astype(BF16), False), 0.0))
    score_n = jnp.where(lane_n <= tq, head_sum(wcol * jnp.maximum(_dot_nt(qi, ki_new), 0.0)), -jnp.inf)
    key_p = _sort_key(score_p)
    key_n = _sort_key(score_n)
    idx_n = past + lane_n

    def count(pred):
        cp = over_pages(jnp.where(pred(key_p, idx_p), 1.0, 0.0), jnp.add)
        return jnp.sum(cp + jnp.where(pred(key_n, idx_n), 1.0, 0.0), axis=1, keepdims=True)

    thr = _search_threshold(lambda cand: count(lambda k, idx: k >= cand), (t_new, 1), float(topk))
    c_gt = count(lambda k, idx: k > thr)
    c_ge = count(lambda k, idx: k >= thr)
    need = float(topk) - c_gt
    j_s[...] = jnp.full((t_new, 1), INT_MAX, I32)

    @pl.when(jnp.max(c_ge - c_gt - need) > 0.0)
    def _():
        j_s[...] = _search_tie_index(lambda cand: count(lambda k, idx: (k == thr) & (idx < cand)),
                                     need, (t_new, 1), n_bits)

    jmax = j_s[...]
    sel = lambda k, idx: (k > thr) | ((k == thr) & (idx <= jmax))
    bias_p = jnp.where(sel(key_p, idx_p), 0.0, NEG)
    bias_n = jnp.where(sel(key_n, idx_n) & (lane_n <= tq), 0.0, NEG)

    q = jnp.concatenate([q_ref[:, h * LANES:(h + 1) * LANES] for h in range(N_HEADS)], axis=0).astype(BF16)
    zpad = jnp.zeros((pad_rows, LANES), F32)
    k_new = jnp.concatenate([k_ref[...], zpad], axis=0).astype(BF16)
    v_new = jnp.concatenate([v_ref[...], zpad], axis=0).astype(BF16)
    s_p = bmm(paged(q), k_buf[slot].astype(BF16), False) + rep(bias_p)
    s_n = _dot_nt(q, k_new) + rep(bias_n)
    m = jnp.max(jnp.maximum(over_pages(s_p, jnp.maximum), s_n), axis=1, keepdims=True)
    p_p = jnp.exp(s_p - m)
    p_n = jnp.exp(s_n - m)
    l = jnp.sum(over_pages(p_p, jnp.add) + p_n, axis=1, keepdims=True)
    o = (over_pages(bmm(p_p.astype(BF16), v_buf[slot].astype(BF16), True), jnp.add)
         + jnp.dot(p_n.astype(BF16), v_new, preferred_element_type=F32)) / l
    heads = [o[h * t_new:(h + 1) * t_new] for h in range(N_HEADS)]

    def store(p, val):
        o_ref[:, p * LANES:(p + 1) * LANES] = val

    _place_heads(store, heads, lane_n)


def _dsa_sample(page_table, q, qi, kw, k, v, cik, ck, cv, topk):
    db, n_pages = page_table.shape
    t_new = q.shape[0] // db
    past = n_pages * PAGE_SIZE
    n_bits = max(1, int(math.ceil(math.log2(past + LANES))) + 1)
    row = lambda width: pl.BlockSpec((t_new, width), lambda b, pt: (b, 0))
    any_spec = pl.BlockSpec(memory_space=pl.ANY)
    return pl.pallas_call(
        functools.partial(_dsa_sample_kernel, n_pages=n_pages, topk=topk, n_bits=n_bits),
        grid_spec=pltpu.PrefetchScalarGridSpec(
            num_scalar_prefetch=1,
            grid=(db,),
            in_specs=[row(N_HEADS * LANES), row(IDX_HEADS * LANES), row(LANES), row(LANES), row(LANES),
                      any_spec, any_spec, any_spec],
            out_specs=row(ATTN_WIDTH),
            scratch_shapes=[pltpu.VMEM((2, n_pages, IDX_DIM, PAGE_SIZE), F32),
                            pltpu.VMEM((2, n_pages, N_KV_HEADS * HEAD_DIM, PAGE_SIZE), F32),
                            pltpu.VMEM((2, n_pages, N_KV_HEADS * HEAD_DIM, PAGE_SIZE), F32),
                            pltpu.SemaphoreType.DMA((3, 2)),
                            pltpu.VMEM((t_new, 1), I32)]),
        out_shape=jax.ShapeDtypeStruct((db * t_new, ATTN_WIDTH), F32),
        compiler_params=pltpu.CompilerParams(dimension_semantics=("arbitrary",), vmem_limit_bytes=56 * 1024 * 1024),
    )(page_table.reshape(-1), q, qi, kw, k, v, cik, ck, cv)


def _outproj_router_kernel(pool_ref, attn_ref, x_ref, wo_ref, g_ref, wr_ref, br_ref, cin_ref,
                           h_ref, xe_ref, meta_ref, cnt_ref, carry_s):
    i = pl.program_id(0)
    tm = x_ref.shape[0]

    @pl.when(i == 0)
    def _():
        carry_s[...] = cin_ref[...]

    h1 = (x_ref[...]
          + jnp.dot(pool_ref[...].astype(BF16), wo_ref[0:POOL_WIDTH, :], preferred_element_type=F32)
          + jnp.dot(attn_ref[...].astype(BF16), wo_ref[POOL_WIDTH:, :], preferred_element_type=F32))
    h_ref[...] = h1
    xn = _rms(h1, g_ref[...])
    for j in range(ROW_TILES):
        xe_ref[pl.ds(j, tm, stride=XE_PITCH), :] = xn[:, j * LANES:(j + 1) * LANES]
    logits = jnp.dot(xn.astype(BF16), wr_ref[...], preferred_element_type=F32) + br_ref[...]

    lane = lax.broadcasted_iota(I32, (tm, LANES), 1).astype(F32)
    rmax = lambda v: jnp.max(v, axis=1, keepdims=True)
    rmin = lambda v: jnp.min(v, axis=1, keepdims=True)
    rsum = lambda v: jnp.sum(v, axis=1, keepdims=True)

    is_g = lane < N_GROUPS
    gl = jnp.where(is_g, logits, -jnp.inf)
    gmax = rmax(gl)
    g_sel = rmin(jnp.where(gl == gmax, lane, float(LANES)))
    g_w = 1.0 / rsum(jnp.where(is_g, jnp.exp(gl - gmax), 0.0))
    e0 = N_GROUPS + g_sel * EXPERTS_PER_GROUP
    is_e = (lane >= e0) & (lane < e0 + EXPERTS_PER_GROUP)
    el = jnp.where(is_e, logits, -jnp.inf)
    un = jnp.where(is_e, jnp.exp(el - rmax(el)), 0.0)
    prob = jnp.where(is_e, un / rsum(un), -1.0)
    p0 = rmax(prob)
    i0 = rmin(jnp.where(prob == p0, lane, float(LANES)))
    prob1 = jnp.where(lane == i0, -1.0, prob)
    p1 = rmax(prob1)
    i1 = rmin(jnp.where(prob1 == p1, lane, float(LANES)))
    w0 = g_w * p0 / (p0 + p1)
    w1 = g_w * p1 / (p0 + p1)
    lo = jnp.minimum(i0, i1) - e0
    hi = jnp.maximum(i0, i1) - e0
    cls = g_sel * PAIRS_PER_GROUP + lo * (2 * EXPERTS_PER_GROUP - 1 - lo) * 0.5 + hi - lo - 1.0
    w_a = jnp.where(i0 < i1, w0, w1)
    w_b = jnp.where(i0 < i1, w1, w0)
    xe_ref[pl.ds(ROW_TILES, tm, stride=XE_PITCH), :] = jnp.where(lane == 0.0, w_a, jnp.where(lane == 1.0, w_b, 0.0))

    onehot = lane == cls
    tri = (lax.broadcasted_iota(I32, (tm, tm), 1) < lax.broadcasted_iota(I32, (tm, tm), 0)).astype(BF16)
    before = jnp.dot(tri, onehot.astype(BF16), preferred_element_type=F32) + carry_s[...]
    rank = rsum(jnp.where(onehot, before, 0.0))
    carry_s[...] = carry_s[...] + jnp.sum(onehot.astype(F32), axis=0, keepdims=True)
    cnt_ref[...] = carry_s[...]
    meta = jnp.where(lane == 0.0, cls, jnp.where(lane == 1.0, rank, 0.0))
    meta_ref[...] = jnp.transpose(meta)[0:8, :].astype(I32)


def _outproj_router(pool_o, attn_o, x, wo, g, wr, br, counts_in):
    n = x.shape[0]
    tm = _tile(n, 512)
    row = lambda width: pl.BlockSpec((tm, width), lambda i: (i, 0))
    full = lambda shape: pl.BlockSpec(shape, lambda i: (0,) * len(shape))
    return pl.pallas_call(
        _outproj_router_kernel,
        grid=(n // tm,),
        in_specs=[row(POOL_WIDTH), row(ATTN_WIDTH), row(D_MODEL), full((D_MODEL, D_MODEL)), full((1, D_MODEL)),
                  full((D_MODEL, LANES)), full((1, LANES)), full((1, LANES))],
        out_specs=[row(D_MODEL), pl.BlockSpec((tm * XE_PITCH, LANES), lambda i: (i, 0)),
                   pl.BlockSpec((8, tm), lambda i: (0, i)), full((1, LANES))],
        out_shape=[jax.ShapeDtypeStruct((n, D_MODEL), F32), jax.ShapeDtypeStruct((n * XE_PITCH, LANES), F32),
                   jax.ShapeDtypeStruct((8, n), I32), jax.ShapeDtypeStruct((1, LANES), F32)],
        scratch_shapes=[pltpu.VMEM((1, LANES), F32)],
        compiler_params=_params(("arbitrary",)),
    )(pool_o, attn_o, x, wo, g, wr, br, counts_in)


def _load_dest(ps_ref, cls_hbm, rank_hbm, cls_s, rank_s, sem, base, tm):
    c1 = pltpu.make_async_copy(cls_hbm.at[pl.ds(base, tm)], cls_s, sem.at[0])
    c2 = pltpu.make_async_copy(rank_hbm.at[pl.ds(base, tm)], rank_s, sem.at[1])
    c1.start()
    c2.start()
    c1.wait()
    c2.wait()
    return lambda r: ps_ref[cls_s[r]] + rank_s[r]


def _row_dmas(tm, make_copy, dest):
    def start(r, carry):
        make_copy(r, dest(r)).start()
        return carry

    def wait(r, carry):
        make_copy(r, 0).wait()
        return carry

    lax.fori_loop(0, tm, start, 0, unroll=8)
    lax.fori_loop(0, tm, wait, 0, unroll=8)


def _dispatch_kernel(ps_ref, cls_hbm, rank_hbm, xe_ref, xs_in_hbm, xs_hbm, cls_s, rank_s, sem, *, tm):
    del xs_in_hbm
    base = pl.program_id(0) * tm
    dest = _load_dest(ps_ref, cls_hbm, rank_hbm, cls_s, rank_s, sem, base, tm)
    _row_dmas(tm, lambda r, d: pltpu.make_async_copy(xe_ref.at[pl.ds(r * XE_PITCH, XE_PITCH)],
                                                     xs_hbm.at[pl.ds(d * XE_PITCH, XE_PITCH)], sem.at[2]), dest)


def _dispatch(pstart, cls, rank, xe, xs):
    n = xe.shape[0] // XE_PITCH
    tm = _tile(n, 512)
    any_spec = pl.BlockSpec(memory_space=pl.ANY)
    return pl.pallas_call(
        functools.partial(_dispatch_kernel, tm=tm),
        grid_spec=pltpu.PrefetchScalarGridSpec(
            num_scalar_prefetch=1, grid=(n // tm,),
            in_specs=[any_spec, any_spec, pl.BlockSpec((tm * XE_PITCH, LANES), lambda i, ps: (i, 0)), any_spec],
            out_specs=any_spec,
            scratch_shapes=[pltpu.SMEM((tm,), I32), pltpu.SMEM((tm,), I32), pltpu.SemaphoreType.DMA((3,))]),
        out_shape=jax.ShapeDtypeStruct(xs.shape, xs.dtype),
        input_output_aliases={4: 0},
        compiler_params=pltpu.CompilerParams(dimension_semantics=("arbitrary",), has_side_effects=True,
                                             vmem_limit_bytes=VMEM_LIMIT),
    )(pstart, cls, rank, xe, xs)


def _ffn_kernel(ba_ref, bb_ref, nu_ref, xs_ref, wga_ref, wua_ref, wda_ref, wgb_ref, wub_ref, wdb_ref, ys_ref):
    j = pl.program_id(0)
    rows = MOE_ROWS

    @pl.when(j < nu_ref[0])
    def _():
        x = jnp.concatenate([xs_ref[pl.ds(t, rows, stride=XE_PITCH), :] for t in range(ROW_TILES)], axis=1).astype(BF16)
        wts = xs_ref[pl.ds(ROW_TILES, rows, stride=XE_PITCH), :]

        def expert(wg_ref, wu_ref, wd_ref):
            gate = jnp.dot(x, wg_ref[0], preferred_element_type=F32)
            up = jnp.dot(x, wu_ref[0], preferred_element_type=F32)
            hid = (gate / (1.0 + jnp.exp(-gate))) * up
            return jnp.dot(hid.astype(BF16), wd_ref[0], preferred_element_type=F32)

        y = wts[:, 0:1] * expert(wga_ref, wua_ref, wda_ref) + wts[:, 1:2] * expert(wgb_ref, wub_ref, wdb_ref)
        for t in range(ROW_TILES):
            ys_ref[pl.ds(t, rows, stride=ROW_TILES), :] = y[:, t * LANES:(t + 1) * LANES]

    @pl.when(j >= nu_ref[0])
    def _():
        ys_ref[...] = jnp.zeros(ys_ref.shape, F32)


def _ffn(blk_a, blk_b, n_used, xs, wg, wu, wd):
    n_blk = xs.shape[0] // (MOE_ROWS * XE_PITCH)
    up_a = pl.BlockSpec((1, D_MODEL, EXPERT_HIDDEN), lambda j, a, b, nu: (a[j], 0, 0))
    up_b = pl.BlockSpec((1, D_MODEL, EXPERT_HIDDEN), lambda j, a, b, nu: (b[j], 0, 0))
    dn_a = pl.BlockSpec((1, EXPERT_HIDDEN, D_MODEL), lambda j, a, b, nu: (a[j], 0, 0))
    dn_b = pl.BlockSpec((1, EXPERT_HIDDEN, D_MODEL), lambda j, a, b, nu: (b[j], 0, 0))
    return pl.pallas_call(
        _ffn_kernel,
        grid_spec=pltpu.PrefetchScalarGridSpec(
            num_scalar_prefetch=3, grid=(n_blk,),
            in_specs=[pl.BlockSpec((MOE_ROWS * XE_PITCH, LANES), lambda j, a, b, nu: (j, 0)),
                      up_a, up_a, dn_a, up_b, up_b, dn_b],
            out_specs=pl.BlockSpec((MOE_ROWS * ROW_TILES, LANES), lambda j, a, b, nu: (j, 0))),
        out_shape=jax.ShapeDtypeStruct((n_blk * MOE_ROWS * ROW_TILES, LANES), F32),
        compiler_params=_params(("arbitrary",)),
    )(blk_a, blk_b, n_used, xs, wg, wu, wd, wg, wu, wd)


def _combine_kernel(ps_ref, cls_hbm, rank_hbm, ys_hbm, h_ref, g_ref, y_ref, cls_s, rank_s, ybuf, sem, *, tm):
    base = pl.program_id(0) * tm
    dest = _load_dest(ps_ref, cls_hbm, rank_hbm, cls_s, rank_s, sem, base, tm)
    _row_dmas(tm, lambda r, d: pltpu.make_async_copy(ys_hbm.at[pl.ds(d * ROW_TILES, ROW_TILES)],
                                                     ybuf.at[pl.ds(r * ROW_TILES, ROW_TILES)], sem.at[2]), dest)
    y = jnp.concatenate([ybuf[pl.ds(t, tm, stride=ROW_TILES), :] for t in range(ROW_TILES)], axis=1)
    y_ref[...] = _rms(h_ref[...] + y, g_ref[...])


def _combine(pstart, cls, rank, ys, h1, g):
    n = h1.shape[0]
    tm = _tile(n, 512)
    any_spec = pl.BlockSpec(memory_space=pl.ANY)
    return pl.pallas_call(
        functools.partial(_combine_kernel, tm=tm),
        grid_spec=pltpu.PrefetchScalarGridSpec(
            num_scalar_prefetch=1, grid=(n // tm,),
            in_specs=[any_spec, any_spec, any_spec, pl.BlockSpec((tm, D_MODEL), lambda i, ps: (i, 0)),
                      pl.BlockSpec((1, D_MODEL), lambda i, ps: (0, 0))],
            out_specs=pl.BlockSpec((tm, D_MODEL), lambda i, ps: (i, 0)),
            scratch_shapes=[pltpu.SMEM((tm,), I32), pltpu.SMEM((tm,), I32), pltpu.VMEM((tm * ROW_TILES, LANES), F32),
                            pltpu.SemaphoreType.DMA((3,))]),
        out_shape=jax.ShapeDtypeStruct((n, D_MODEL), F32),
        compiler_params=_params(("arbitrary",)),
    )(pstart, cls, rank, ys, h1, g)


def _moe_and_final(pool_o, attn_o, x, wo, g_ffn, wr, br, wg, wu, wd, g_final):
    n = x.shape[0]
    h1, xe, meta, counts = _outproj_router(pool_o, attn_o, x, wo, g_ffn, wr, br, jnp.zeros((1, LANES), F32))
    cnt = counts[0].astype(I32)
    padded = ((cnt + MOE_ROWS - 1) // MOE_ROWS) * MOE_ROWS
    pends = jnp.cumsum(padded)
    pstart = (pends - padded).astype(I32)
    n_blk = -(-n // MOE_ROWS) + N_CLASSES
    n_used = (pends[-1] // MOE_ROWS).astype(I32)
    blk = jnp.arange(n_blk, dtype=I32)
    blk_last = jnp.minimum(blk, jnp.maximum(n_used - 1, 0))
    blk_cls = jnp.sum((pends[None, :N_CLASSES] <= (blk_last * MOE_ROWS)[:, None]).astype(I32), axis=1)
    blk_cls = jnp.minimum(blk_cls, N_CLASSES - 1)
    blk_a = jnp.asarray(CLASS_A)[blk_cls]
    blk_b = jnp.asarray(CLASS_B)[blk_cls]
    cls, rank = meta[0], meta[1]
    xs = _dispatch(pstart, cls, rank, xe, jnp.zeros((n_blk * MOE_ROWS * XE_PITCH, LANES), F32))
    ys = _ffn(blk_a, blk_b, n_used.reshape(1), xs, wg, wu, wd)
    return _combine(pstart, cls, rank, ys, h1, g_final)


def kernel(x_prompt, x_sample, cache_k, cache_v, cache_idx_k, state_pool, page_table, norm_mix_g, w_in, pool_w,
           pool_scale, w_out, norm_ffn_g, router_group_w, router_group_b, router_expert_w, router_expert_b,
           expert_w_gate, expert_w_up, expert_w_down, final_norm_g):
    depth = w_in.shape[0]
    assert depth == 1, "single-layer step"
    b, s, d = x_prompt.shape
    db, t_new, _ = x_sample.shape
    n_pages = page_table.shape[1]
    past = n_pages * PAGE_SIZE
    n_pool = cache_k.shape[1]
    l = 0

    w_in_b = jnp.pad(w_in[l], ((0, 0), (0, IN_PAD - IN_WIDTH))).astype(BF16)
    g_mix = norm_mix_g[l].reshape(1, d)
    pw = pool_w[l].astype(BF16)
    psc = pool_scale[l].reshape(1, POOL_WIDTH)
    wo = w_out[l].astype(BF16)
    g_ffn = norm_ffn_g[l].reshape(1, d)
    wr = jnp.pad(jnp.concatenate([router_group_w[l], router_expert_w[l]], axis=1),
                 ((0, 0), (0, LANES - N_GROUPS - N_EXPERTS))).astype(BF16)
    br = jnp.pad(jnp.concatenate([router_group_b[l], router_expert_b[l]]), (0, LANES - N_GROUPS - N_EXPERTS)).reshape(1, LANES)
    wg = expert_w_gate[l].astype(BF16)
    wu = expert_w_up[l].astype(BF16)
    wd = expert_w_down[l].astype(BF16)
    g_fin = final_norm_g.reshape(1, d)
    moe = lambda pool_o, attn_o, x: _moe_and_final(pool_o, attn_o, x, wo, g_ffn, wr, br, wg, wu, wd, g_fin)

    xp = x_prompt.reshape(b * s, d)
    tab_p = _rope_tables(jnp.arange(s, dtype=I32))
    u, qt, k, v, qit, ki, kw, kb, vt, kwb = _inproj(xp, g_mix, w_in_b, tab_p, seq_len=s)
    r3 = lambda a: a.reshape(b, s, a.shape[-1])
    pool_o = _pool_prompt(r3(u), pw, psc)
    attn_o = _dsa_prompt(qt, qit, r3(kw), r3(kwb), r3(kb), vt, min(TOPK_MAX, s // 4))
    y_prompt = moe(pool_o.reshape(b * s, POOL_WIDTH), attn_o.reshape(b * s, ATTN_WIDTH), xp).reshape(b, s, d)
    k_prompt = jnp.moveaxis(k.reshape(b, N_KV_HEADS, HEAD_DIM, s), -1, 1)[None]
    v_prompt = jnp.moveaxis(v.reshape(b, N_KV_HEADS, HEAD_DIM, s), -1, 1)[None]
    idx_k_prompt = jnp.moveaxis(ki, -1, 1)[None]
    if s >= POOL_STATE:
        pool_prompt = r3(u)[:, s - POOL_STATE:][None]
    else:
        pool_prompt = jnp.concatenate([jnp.zeros((b, POOL_STATE, POOL_WIDTH), F32), r3(u)], axis=1)[:, -POOL_STATE:][None]

    xs_ = x_sample.reshape(db * t_new, d)
    tab_s = _rope_tables(past + (jnp.arange(db * t_new, dtype=I32) % t_new))
    u2, q2, k2, v2, qi2, ki2, kw2, _, _, _ = _inproj(xs_, g_mix, w_in_b, tab_s)
    u2_3 = u2.reshape(db, t_new, POOL_WIDTH)
    pool2 = _pool_sample(jnp.swapaxes(state_pool[l], 0, 1), jnp.swapaxes(u2_3, 0, 1), pw, psc, past)
    pool2 = jnp.swapaxes(pool2, 0, 1).reshape(db * t_new, POOL_WIDTH)
    page_t = lambda c: jnp.moveaxis(c[l], 1, -1).reshape(n_pool, -1, PAGE_SIZE)
    attn2 = _dsa_sample(page_table, q2, qi2, kw2, k2, v2, page_t(cache_idx_k), page_t(cache_k), page_t(cache_v),
                        min(TOPK_MAX, (past + t_new) // 4))
    y_sample = moe(pool2, attn2, xs_).reshape(db, t_new, d)
    k_sample = k2.reshape(1, db, t_new, N_KV_HEADS, HEAD_DIM)
    v_sample = v2.reshape(1, db, t_new, N_KV_HEADS, HEAD_DIM)
    idx_k_sample = ki2.reshape(1, db, t_new, IDX_DIM)
    pool_sample = jnp.concatenate([state_pool[l], u2_3], axis=1)[:, -POOL_STATE:][None]

    return (y_prompt, y_sample, k_prompt, v_prompt, idx_k_prompt, pool_prompt,
            k_sample, v_sample, idx_k_sample, pool_sample)
```

```python
import functools
import math

import numpy as np
import jax
import jax.numpy as jnp
from jax import lax
from jax.experimental import pallas as pl
from jax.experimental.pallas import tpu as pltpu

F32 = jnp.float32
BF16 = jnp.bfloat16
I32 = jnp.int32

D_MODEL = 1024
HEAD_DIM = 64
N_HEADS = 8
N_KV_HEADS = 2
ROPE_DIM = 16
ROPE_THETA = 500000.0
IDX_HEADS = 8
IDX_DIM = 64
TOPK_MAX = 256
PAGE_SIZE = 128
POOL_WIDTH = 512
POOL_WINDOWS = (2, 4, 8, 16)
POOL_GROUP = 128
POOL_STATE = 15
ATTN_WIDTH = 512
N_GROUPS = 4
EXPERTS_PER_GROUP = 8
N_EXPERTS = 32
EXPERT_HIDDEN = 256
NORM_EPS = 1e-6

LANES = 128
IN_WIDTH = 1864
IN_PAD = 1920
C_U, C_Q, C_K, C_V, C_QI, C_KW = 0, 512, 1024, 1152, 1280, 1792
PAIRS_PER_GROUP = EXPERTS_PER_GROUP * (EXPERTS_PER_GROUP - 1) // 2
N_CLASSES = N_GROUPS * PAIRS_PER_GROUP
MOE_ROWS = 128
ROW_TILES = D_MODEL // LANES
XE_PITCH = ROW_TILES + 1
ATT_SUB_HEADS = 4
NEG = -1e30
INT_MIN = -2 ** 31
INT_MAX = 2 ** 31 - 1
VMEM_LIMIT = 48 * 1024 * 1024


def _class_tables():
    a, b = [], []
    for g in range(N_GROUPS):
        for lo in range(EXPERTS_PER_GROUP):
            for hi in range(lo + 1, EXPERTS_PER_GROUP):
                a.append(g * EXPERTS_PER_GROUP + lo)
                b.append(g * EXPERTS_PER_GROUP + hi)
    return np.array(a, np.int32), np.array(b, np.int32)


CLASS_A, CLASS_B = _class_tables()


def _tile(n, cap):
    t = min(n, cap)
    while n % t:
        t -= 8
    return t


def _params(sem):
    return pltpu.CompilerParams(dimension_semantics=sem, vmem_limit_bytes=VMEM_LIMIT)


def _rms(x, g):
    return x * lax.rsqrt(jnp.mean(x * x, axis=-1, keepdims=True) + NORM_EPS) * g


def _dot_nt(a, b):
    return lax.dot_general(a, b, (((1,), (1,)), ((), ())), preferred_element_type=F32)


def _tree(parts, op):
    parts = list(parts)
    while len(parts) > 1:
        nxt = [op(parts[a], parts[a + 1]) for a in range(0, len(parts) - 1, 2)]
        if len(parts) % 2:
            nxt.append(parts[-1])
        parts = nxt
    return parts[0]


def _fold_rows(x, op):
    return _tree([x[r:r + 8] for r in range(0, x.shape[0], 8)], op)


def _count_rows(mask, n_acc=8):
    accs = [jnp.zeros((8, mask.shape[1]), F32) for _ in range(n_acc)]
    for a, r in enumerate(range(0, mask.shape[0], 8)):
        acc = accs[a % n_acc]
        accs[a % n_acc] = jnp.where(mask[r:r + 8], acc + 1.0, acc)
    return _fold_rows(jnp.concatenate(accs, axis=0), jnp.add)


def _sort_key(score):
    bits = lax.bitcast_convert_type(score, I32)
    return jnp.where(bits < 0, bits ^ jnp.int32(INT_MAX), bits)


def _rope_tables(pos):
    half = ROPE_DIM // 2
    r = pos.shape[0]
    inv = jnp.power(jnp.float32(ROPE_THETA), -jnp.arange(half, dtype=F32) / half)
    ang = pos.astype(F32)[:, None] * inv[None, :]
    cos, sin = jnp.cos(ang), jnp.sin(ang)
    one = lambda n: jnp.ones((r, n), F32)
    zero = lambda n: jnp.zeros((r, n), F32)
    c64 = jnp.concatenate([cos, cos, one(HEAD_DIM - ROPE_DIM)], axis=1)
    s1_64 = jnp.concatenate([-sin, zero(HEAD_DIM - half)], axis=1)
    s2_64 = jnp.concatenate([zero(half), sin, zero(HEAD_DIM - ROPE_DIM)], axis=1)
    ck = jnp.concatenate([c64, jnp.full((r, IDX_HEADS), IDX_HEADS ** -0.5, F32), one(LANES - HEAD_DIM - IDX_HEADS)], axis=1)
    return jnp.concatenate([c64, c64, s1_64, s1_64, s2_64, s2_64, ck, s1_64, zero(64), s2_64, zero(64)], axis=1)


def _inproj_kernel(x_ref, g_ref, w_ref, tab_ref, u_ref, q_ref, k_ref, v_ref, qi_ref, ki_ref, kw_ref,
                   kb_ref, vb_ref, kwb_ref, *, transposed):
    xb = _rms(x_ref[...], g_ref[...]).astype(BF16)
    tm = xb.shape[0]
    lane = lax.broadcasted_iota(I32, (tm, LANES), 1)

    def seg(c0, c1):
        return jnp.dot(xb, w_ref[:, c0:c1], preferred_element_type=F32)

    def rope(z, t0):
        c, s1, s2 = (tab_ref[:, t0 + i * LANES:t0 + (i + 1) * LANES] for i in range(3))
        return z * c + pltpu.roll(z, LANES - ROPE_DIM // 2, 1) * s1 + pltpu.roll(z, ROPE_DIM // 2, 1) * s2

    u_ref[...] = seg(C_U, C_Q)

    for c_base, out_ref, to_group in ((C_Q, q_ref, True), (C_QI, qi_ref, False)):
        z = seg(c_base, c_base + N_HEADS * HEAD_DIM)
        for p in range(N_HEADS // 2):
            zz = rope(z[:, p * LANES:(p + 1) * LANES], 0) * (HEAD_DIM ** -0.5)
            zr = pltpu.roll(zz, HEAD_DIM, 1)
            for hh in range(2):
                h = 2 * p + hh
                tgt = (h // (N_HEADS // N_KV_HEADS)) if to_group else 0
                src = zz if hh == tgt else zr
                keep = (lane >= tgt * HEAD_DIM) & (lane < (tgt + 1) * HEAD_DIM)
                val = jnp.where(keep, src, 0.0)
                if transposed:
                    for qb in range(tm // LANES):
                        blk = jnp.transpose(val[qb * LANES:(qb + 1) * LANES])
                        out_ref[0, qb, :, h * LANES:(h + 1) * LANES] = blk.astype(out_ref.dtype)
                else:
                    out_ref[:, h * LANES:(h + 1) * LANES] = val.astype(out_ref.dtype)

    k = rope(seg(C_K, C_V), 0)
    kb_ref[...] = k.astype(BF16)
    v = seg(C_V, C_QI)
    kw = rope(seg(C_KW, IN_PAD), 3 * LANES)
    kw_ref[...] = kw
    kwb_ref[...] = kw.astype(BF16)
    if transposed:
        for qb in range(tm // LANES):
            cols = slice(qb * LANES, (qb + 1) * LANES)
            k_ref[0, :, cols] = jnp.transpose(k[cols])
            vt = jnp.transpose(v[cols])
            v_ref[0, :, cols] = vt
            vb_ref[0, 0, :, cols] = vt.astype(BF16)
            ki_ref[0, :, cols] = jnp.transpose(kw[cols])[:IDX_DIM]
    else:
        k_ref[...] = k
        v_ref[...] = v
        vb_ref[...] = v.astype(BF16)
        ki_ref[...] = kw[:, :IDX_DIM]


def _inproj(x, g, w, tab, seq_len=None):
    n = x.shape[0]
    transposed = seq_len is not None
    tm = _tile(seq_len, 512) if transposed else _tile(n, 512)
    n_tab = tab.shape[0] // tm
    row = lambda width: pl.BlockSpec((tm, width), lambda i: (i, 0))
    rows = lambda width, dt: (row(width), jax.ShapeDtypeStruct((n, width), dt))
    if transposed:
        assert tm % LANES == 0
        nb, ns, qpt = n // seq_len, seq_len // tm, tm // LANES
        qt = (pl.BlockSpec((1, qpt, LANES, N_HEADS * LANES), lambda i: (i // ns, i % ns, 0, 0)),
              jax.ShapeDtypeStruct((nb, seq_len // LANES, LANES, N_HEADS * LANES), BF16))
        vt = (pl.BlockSpec((1, 1, LANES, tm), lambda i: (i // ns, i % ns, 0, 0)),
              jax.ShapeDtypeStruct((nb, ns, LANES, tm), BF16))
        tr = lambda width: (pl.BlockSpec((1, width, tm), lambda i: (i // ns, 0, i % ns)),
                            jax.ShapeDtypeStruct((nb, width, seq_len), F32))
        kf, vf, kif = tr(LANES), tr(LANES), tr(IDX_DIM)
    else:
        qt = rows(N_HEADS * LANES, F32)
        vt = rows(LANES, BF16)
        kf, vf, kif = rows(LANES, F32), rows(LANES, F32), rows(IDX_DIM, F32)
    outs = [rows(POOL_WIDTH, F32), qt, kf, vf, qt, kif, rows(LANES, F32), rows(LANES, BF16), vt, rows(LANES, BF16)]
    return pl.pallas_call(
        functools.partial(_inproj_kernel, transposed=transposed),
        grid=(n // tm,),
        in_specs=[row(D_MODEL), pl.BlockSpec((1, D_MODEL), lambda i: (0, 0)),
                  pl.BlockSpec((D_MODEL, IN_PAD), lambda i: (0, 0)),
                  pl.BlockSpec((tm, 6 * LANES), lambda i: (i % n_tab, 0))],
        out_specs=[spec for spec, _ in outs],
        out_shape=[shape for _, shape in outs],
        compiler_params=_params(("arbitrary",)),
    )(x, g, w, tab)


def _pool_prompt_kernel(u_ref, pw_ref, sc_ref, o_ref, ext_ref):
    t = pl.program_id(1)
    tp = u_ref.shape[1]
    halo = POOL_STATE + 1

    @pl.when(t == 0)
    def _():
        ext_ref[0:halo, :] = jnp.zeros((halo, POOL_WIDTH), F32)

    @pl.when(t > 0)
    def _():
        ext_ref[0:halo, :] = ext_ref[tp:tp + halo, :]

    ext_ref[halo:halo + tp, :] = u_ref[0]
    pos = t * tp + lax.broadcasted_iota(I32, (tp, 1), 0)
    for g, w in enumerate(POOL_WINDOWS):
        c0 = g * POOL_GROUP
        cur = ext_ref[halo:halo + tp, c0:c0 + POOL_GROUP]
        win = cur
        for j in range(1, w):
            win = win + ext_ref[halo - j:halo - j + tp, c0:c0 + POOL_GROUP]
        cnt = jnp.minimum(pos + 1, w).astype(F32)
        d = (win / cnt - cur).astype(BF16)
        o = jnp.dot(d, pw_ref[g], preferred_element_type=F32) * sc_ref[:, c0:c0 + POOL_GROUP]
        o_ref[0, :, c0:c0 + POOL_GROUP] = o.astype(o_ref.dtype)


def _pool_prompt(u, pw, sc):
    b, s, _ = u.shape
    tp = _tile(s, 512)
    return pl.pallas_call(
        _pool_prompt_kernel,
        grid=(b, s // tp),
        in_specs=[pl.BlockSpec((1, tp, POOL_WIDTH), lambda i, t: (i, t, 0)),
                  pl.BlockSpec((len(POOL_WINDOWS), POOL_GROUP, POOL_GROUP), lambda i, t: (0, 0, 0)),
                  pl.BlockSpec((1, POOL_WIDTH), lambda i, t: (0, 0))],
        out_specs=pl.BlockSpec((1, tp, POOL_WIDTH), lambda i, t: (i, t, 0)),
        out_shape=jax.ShapeDtypeStruct((b, s, POOL_WIDTH), BF16),
        scratch_shapes=[pltpu.VMEM((tp + POOL_STATE + 1, POOL_WIDTH), F32)],
        compiler_params=_params(("arbitrary", "arbitrary")),
    )(u, pw, sc)


def _pool_sample_kernel(st_ref, u_ref, pw_ref, sc_ref, o_ref, *, pos0):
    t_new = u_ref.shape[0]
    ext = [st_ref[i] for i in range(POOL_STATE)] + [u_ref[i] for i in range(t_new)]
    for g, w in enumerate(POOL_WINDOWS):
        c0 = g * POOL_GROUP
        ds = []
        for t in range(t_new):
            e = POOL_STATE + t
            cur = ext[e][:, c0:c0 + POOL_GROUP]
            win = cur
            for j in range(1, w):
                win = win + ext[e - j][:, c0:c0 + POOL_GROUP]
            cnt = float(min(pos0 + t + 1, w))
            ds.append((win / cnt - cur).astype(BF16))
        d = jnp.concatenate(ds, axis=0)
        o = jnp.dot(d, pw_ref[g], preferred_element_type=F32) * sc_ref[:, c0:c0 + POOL_GROUP]
        db = o.shape[0] // t_new
        for t in range(t_new):
            o_ref[t, :, c0:c0 + POOL_GROUP] = o[t * db:(t + 1) * db]


def _pool_sample(st_t, u_t, pw, sc, pos0):
    t_new, db, _ = u_t.shape
    return pl.pallas_call(
        functools.partial(_pool_sample_kernel, pos0=pos0),
        out_shape=jax.ShapeDtypeStruct((t_new, db, POOL_WIDTH), F32),
        compiler_params=pltpu.CompilerParams(vmem_limit_bytes=VMEM_LIMIT),
    )(st_t, u_t, pw, sc)


def _search_threshold(count_ge, shape, topk):
    def bit_body(it, t):
        cand = t + lax.shift_left(jnp.int32(1), 31 - it)
        return jnp.where(count_ge(cand) >= topk, cand, t)
    return lax.fori_loop(0, 32, bit_body, jnp.full(shape, INT_MIN, I32))


def _search_threshold16(count_ge, need, width):
    def bit_body(it, t):
        cand = t + lax.shift_left(jnp.int32(1), 15 - it)
        return jnp.where(count_ge(cand) >= need, cand, t)
    return lax.fori_loop(0, 16, bit_body, jnp.full((1, width), -32768, I32))


def _pack_halves(x):
    half = x.shape[0] // 2
    return (x[:half] & 0xFFFF) | (x[half:] << 16)


def _search_tie_index(count_eq_below, need, shape, n_bits):
    def bit_body(it, j):
        cand = j + lax.shift_left(jnp.int32(1), n_bits - 1 - it)
        return jnp.where(count_eq_below(cand) < need, cand, j)
    return lax.fori_loop(0, n_bits, bit_body, jnp.zeros(shape, I32))


def _place_heads(o_ref_store, heads, lane):
    for p in range(N_HEADS // 2):
        parts = []
        for hh in range(2):
            h = 2 * p + hh
            g = h // (N_HEADS // N_KV_HEADS)
            parts.append(heads[h] if g == hh else pltpu.roll(heads[h], HEAD_DIM, 1))
        o_ref_store(p, jnp.where(lane < HEAD_DIM, parts[0], parts[1]))


def _dsa_prompt_kernel(qt_ref, qit_ref, kwq_ref, kwb_ref, kb_ref, vt_ref, o_ref, key_s, hi_s, lo_s, j_s,
                       *, topk, ck, n_bits):
    i = pl.program_id(1)
    tq = LANES
    nc = (i * tq) // ck + 1
    qpos = i * tq + lax.broadcasted_iota(I32, (ck, tq), 1)
    row = lax.broadcasted_iota(I32, (ck, tq), 0)
    hpg = N_HEADS // N_KV_HEADS

    wit =jnp.transpose(kwq_ref[0])[IDX_DIM:IDX_DIM + IDX_HEADS, :]

    def score_body(c, carry):
        kc = kwb_ref[0, pl.ds(pl.multiple_of(c * ck, ck), ck), :]
        st = jnp.dot(kc, qit_ref[0, 0], preferred_element_type=F32)
        acc = jnp.zeros((ck, tq), F32)
        for h in range(IDX_HEADS):
            acc = acc + wit[h:h + 1, :] * jnp.maximum(st[:, h * tq:(h + 1) * tq], 0.0)
        score = jnp.where(c * ck + row <= qpos, acc + 0.0, -jnp.inf)
        key = _sort_key(score)
        key_s[c] = key
        hi_s[c] = _pack_halves(key >> 16)
        return carry

    lax.fori_loop(0, nc, score_body, 0)

    def count(pred):
        def body(c, acc):
            return acc + _count_rows(pred(key_s[c], c * ck + row))
        acc = lax.fori_loop(0, nc, body, jnp.zeros((8, tq), F32))
        return jnp.sum(acc, axis=0, keepdims=True)

    def count16(packed_s, cand, strict=False):
        c16 = pltpu.bitcast(jnp.broadcast_to(_pack_halves(jnp.concatenate([cand, cand], axis=0)), (8, tq)), jnp.int16)

        def body(c, accs):
            x16 = pltpu.bitcast(packed_s[c], jnp.int16)
            accs = list(accs)
            for a, r in enumerate(range(0, ck, 16)):
                x = x16[r:r + 16]
                hit = (x > c16) if strict else (x >= c16)
                accs[a % 8] = jnp.where(hit, accs[a % 8] + jnp.int16(1), accs[a % 8])
            return tuple(accs)

        accs = lax.fori_loop(0, nc, body, tuple(jnp.zeros((16, tq), jnp.int16) for _ in range(8)))
        w = pltpu.bitcast(_tree(accs, jnp.add), I32)
        return jnp.sum((((w << 16) >> 16) + (w >> 16)).astype(F32), axis=0, keepdims=True)

    t_hi = _search_threshold16(lambda cand: count16(hi_s, cand), float(topk), tq)
    c_above = count16(hi_s, t_hi, strict=True)

    def band_body(c, carry):
        key = key_s[c]
        lo = (key & 0xFFFF) - 32768
        lo_s[c] = _pack_halves(jnp.where((key >> 16) == t_hi, lo, -32768))
        return carry

    lax.fori_loop(0, nc, band_body, 0)
    t_lo = _search_threshold16(lambda cand: count16(lo_s, cand), float(topk) - c_above, tq)
    thr = t_hi * 65536 + (t_lo + 32768)
    c_gt = c_above + count16(lo_s, t_lo, strict=True)
    c_ge = c_above + count16(lo_s, t_lo)
    need = float(topk) - c_gt
    j_s[...] = jnp.full((1, tq), INT_MAX, I32)

    @pl.when(jnp.max(c_ge - c_gt - need) > 0.0)
    def _():
        j_s[...] = _search_tie_index(lambda cand: count(lambda k, idx: (k == thr) & (idx < cand)),
                                     need, (1, tq), n_bits)

    jmax = j_s[...]

    hps = ATT_SUB_HEADS
    sw = hps * tq
    n_sub = N_HEADS // hps

    def att_body(c, carry):
        off = pl.multiple_of(c * ck, ck)
        k = key_s[c]
        idx = c * ck + row
        sel = (k > thr) | ((k == thr) & (idx <= jmax))
        b = jnp.where(sel & (idx <= qpos), 0.0, NEG)
        bias = jnp.concatenate([b] * hps, axis=1)
        kc = kb_ref[0, pl.ds(off, ck), :]
        vc = vt_ref[0, c]
        out = []
        for sb in range(n_sub):
            m, l, acc = carry[sb]
            st = jnp.dot(kc, qt_ref[0, 0, :, sb * sw:(sb + 1) * sw], preferred_element_type=F32) + bias
            m_new = jnp.maximum(m, jnp.max(_fold_rows(st, jnp.maximum), axis=0, keepdims=True))
            alpha = jnp.exp(m - m_new)
            p = jnp.exp(st - m_new)
            l = alpha * l + jnp.sum(_fold_rows(p, jnp.add), axis=0, keepdims=True)
            acc = alpha * acc + jnp.dot(vc, p.astype(BF16), preferred_element_type=F32)
            out.append((m_new, l, acc))
        return tuple(out)

    init = (jnp.full((1, sw), NEG, F32), jnp.zeros((1, sw), F32), jnp.zeros((LANES, sw), F32))
    res = lax.fori_loop(0, nc, att_body, (init,) * n_sub)
    for sb in range(n_sub):
        _, l, acc = res[sb]
        g = (sb * hps) // hpg
        ot = (acc / l)[g * HEAD_DIM:(g + 1) * HEAD_DIM, :]
        for r in range(0, hps, 2):
            pair = jnp.concatenate([ot[:, r * tq:(r + 1) * tq], ot[:, (r + 1) * tq:(r + 2) * tq]], axis=0)
            c0 = (sb * hps + r) * HEAD_DIM
            o_ref[0, :, c0:c0 + LANES] = jnp.transpose(pair).astype(o_ref.dtype)


def _dsa_prompt(qt, qit, kw, kwb, kb, vt, topk):
    b, s, _ = kb.shape
    tq = LANES
    ck = vt.shape[-1]
    assert ck >= topk and ck % tq == 0 and s % ck == 0
    n_bits = max(1, int(math.ceil(math.log2(s))) + 1)
    qspec = pl.BlockSpec((1, 1, LANES, N_HEADS * LANES), lambda bi, i: (bi, i, 0, 0))
    kspec = pl.BlockSpec((1, s, LANES), lambda bi, i: (bi, 0, 0))
    return pl.pallas_call(
        functools.partial(_dsa_prompt_kernel, topk=topk, ck=ck, n_bits=n_bits),
        grid=(b, s // tq),
        in_specs=[qspec, qspec, pl.BlockSpec((1, tq, LANES), lambda bi, i: (bi, i, 0)), kspec, kspec,
                  pl.BlockSpec((1, s // ck, LANES, ck), lambda bi, i: (bi, 0, 0, 0))],
        out_specs=pl.BlockSpec((1, tq, ATTN_WIDTH), lambda bi, i: (bi, i, 0)),
        out_shape=jax.ShapeDtypeStruct((b, s, ATTN_WIDTH), BF16),
        scratch_shapes=[pltpu.VMEM((s // ck, ck, tq), I32), pltpu.VMEM((s // ck, ck // 2, tq), I32),
                        pltpu.VMEM((s // ck, ck // 2, tq), I32), pltpu.VMEM((1, tq), I32)],
        compiler_params=_params(("arbitrary", "arbitrary")),
    )(qt, qit, kw, kwb, kb, vt)


def _dsa_sample_kernel(pt_ref, q_ref, qi_ref, kw_ref, k_ref, v_ref, cik_hbm, ck_hbm, cv_hbm, o_ref,
                       ki_buf, k_buf, v_buf, sems, j_s, *, n_pages, topk, n_bits):
    b = pl.program_id(0)
    nb = pl.num_programs(0)
    slot = b % 2
    t_new = q_ref.shape[0]
    past = n_pages * PAGE_SIZE
    srcs = (cik_hbm, ck_hbm, cv_hbm)
    bufs = (ki_buf, k_buf, v_buf)

    def page_copy(a, page, sl, p):
        return pltpu.make_async_copy(srcs[a].at[page], bufs[a].at[sl, p], sems.at[a, sl])

    def start_all(bb, sl):
        def body(p, carry):
            page = pt_ref[bb * n_pages + p]
            for a in range(3):
                page_copy(a, page, sl, p).start()
            return carry
        lax.fori_loop(0, n_pages, body, 0)

    def wait_all(sl):
        def body(p, carry):
            for a in range(3):
                page_copy(a, 0, sl, p).wait()
            return carry
        lax.fori_loop(0, n_pages, body, 0)

    @pl.when(b == 0)
    def _():
        start_all(0, 0)

    @pl.when(b + 1 < nb)
    def _():
        start_all(b + 1, 1 - slot)

    wait_all(slot)

    pad_rows = LANES - t_new
    tq = lax.broadcasted_iota(I32, (t_new, 1), 0)
    lane_n = lax.broadcasted_iota(I32, (t_new, LANES), 1)
    idx_p = (lax.broadcasted_iota(I32, (n_pages, t_new, PAGE_SIZE), 0) * PAGE_SIZE
             + lax.broadcasted_iota(I32, (n_pages, t_new, PAGE_SIZE), 2))
    rep = lambda x: jnp.concatenate([x] * N_HEADS, axis=x.ndim - 2)
    paged = lambda a: jnp.broadcast_to(a[None], (n_pages,) + a.shape)
    bmm = lambda a, b_, nt: lax.dot_general(a, b_, (((2,), (2 if nt else 1,)), ((0,), (0,))),
                                            preferred_element_type=F32)
    over_pages = lambda x, op: _tree([x[p] for p in range(n_pages)], op)

    qi = jnp.concatenate([qi_ref[:, h * LANES:h * LANES + IDX_DIM] for h in range(IDX_HEADS)], axis=0).astype(BF16)
    wi = kw_ref[:, IDX_DIM:IDX_DIM + IDX_HEADS]
    wcol = jnp.concatenate([wi[:, h:h + 1] for h in range(IDX_HEADS)], axis=0)
    ki_new = jnp.concatenate([kw_ref[:, :IDX_DIM], jnp.zeros((pad_rows, IDX_DIM), F32)], axis=0).astype(BF16)

    def head_sum(r):
        acc = r[..., 0:t_new, :]
        for h in range(1, IDX_HEADS):
            acc = acc + r[..., h * t_new:(h + 1) * t_new, :]
        return acc + 0.0

    score_p = head_sum(wcol * jnp.maximum(bmm(paged(qi), ki_buf[slot].astype(BF16), False), 0.0))
    score_n = jnp.where(lane_n <= tq, head_sum(wcol * jnp.maximum(_dot_nt(qi, ki_new), 0.0)), -jnp.inf)
    key_p = _sort_key(score_p)
    key_n = _sort_key(score_n)
    idx_n = past + lane_n

    def count(pred):
        cp = over_pages(jnp.where(pred(key_p, idx_p), 1.0, 0.0), jnp.add)
        return jnp.sum(cp + jnp.where(pred(key_n, idx_n), 1.0, 0.0), axis=1, keepdims=True)

    thr = _search_threshold(lambda cand: count(lambda k, idx: k >= cand), (t_new, 1), float(topk))
    c_gt = count(lambda k, idx: k > thr)
    c_ge = count(lambda k, idx: k >= thr)
    need = float(topk) - c_gt
    j_s[...] = jnp.full((t_new, 1), INT_MAX, I32)

    @pl.when(jnp.max(c_ge - c_gt - need) > 0.0)
    def _():
        j_s[...] = _search_tie_index(lambda cand: count(lambda k, idx: (k == thr) & (idx < cand)),
                                     need, (t_new, 1), n_bits)

    jmax = j_s[...]
    sel = lambda k, idx: (k > thr) | ((k == thr) & (idx <= jmax))
    bias_p = jnp.where(sel(key_p, idx_p), 0.0, NEG)
    bias_n = jnp.where(sel(key_n, idx_n) & (lane_n <= tq), 0.0, NEG)

    q = jnp.concatenate([q_ref[:, h * LANES:(h + 1) * LANES] for h in range(N_HEADS)], axis=0).astype(BF16)
    zpad = jnp.zeros((pad_rows, LANES), F32)
    k_new = jnp.concatenate([k_ref[...], zpad], axis=0).astype(BF16)
    v_new = jnp.concatenate([v_ref[...], zpad], axis=0).astype(BF16)
    s_p = bmm(paged(q), k_buf[slot].astype(BF16), False) + rep(bias_p)
    s_n = _dot_nt(q, k_new) + rep(bias_n)
    m = jnp.max(jnp.maximum(over_pages(s_p, jnp.maximum), s_n), axis=1, keepdims=True)
    p_p = jnp.exp(s_p - m)
    p_n = jnp.exp(s_n - m)
    l = jnp.sum(over_pages(p_p, jnp.add) + p_n, axis=1, keepdims=True)
    o = (over_pages(bmm(p_p.astype(BF16), v_buf[slot].astype(BF16), True), jnp.add)
         + jnp.dot(p_n.astype(BF16), v_new, preferred_element_type=F32)) / l
    heads = [o[h * t_new:(h + 1) * t_new] for h in range(N_HEADS)]

    def store(p, val):
        o_ref[:, p * LANES:(p + 1) * LANES] = val

    _place_heads(store, heads, lane_n)


def _dsa_sample(page_table, q, qi, kw, k, v, cik, ck, cv, topk):
    db, n_pages = page_table.shape
    t_new = q.shape[0] // db
    past = n_pages * PAGE_SIZE
    n_bits = max(1, int(math.ceil(math.log2(past + LANES))) + 1)
    row = lambda width: pl.BlockSpec((t_new, width), lambda b, pt: (b, 0))
    any_spec = pl.BlockSpec(memory_space=pl.ANY)
    return pl.pallas_call(
        functools.partial(_dsa_sample_kernel, n_pages=n_pages, topk=topk, n_bits=n_bits),
        grid_spec=pltpu.PrefetchScalarGridSpec(
            num_scalar_prefetch=1,
            grid=(db,),
            in_specs=[row(N_HEADS * LANES), row(IDX_HEADS * LANES), row(LANES), row(LANES), row(LANES),
                      any_spec, any_spec, any_spec],
            out_specs=row(ATTN_WIDTH),
            scratch_shapes=[pltpu.VMEM((2, n_pages, IDX_DIM, PAGE_SIZE), F32),
                            pltpu.VMEM((2, n_pages, N_KV_HEADS * HEAD_DIM, PAGE_SIZE), F32),
                            pltpu.VMEM((2, n_pages, N_KV_HEADS * HEAD_DIM, PAGE_SIZE), F32),
                            pltpu.SemaphoreType.DMA((3, 2)),
                            pltpu.VMEM((t_new, 1), I32)]),
        out_shape=jax.ShapeDtypeStruct((db * t_new, ATTN_WIDTH), F32),
        compiler_params=pltpu.CompilerParams(dimension_semantics=("arbitrary",), vmem_limit_bytes=56 * 1024 * 1024),
    )(page_table.reshape(-1), q, qi, kw, k, v, cik, ck, cv)


def _outproj_router_kernel(pool_ref, attn_ref, x_ref, wo_ref, g_ref, wr_ref, br_ref, cin_ref,
                           h_ref, xe_ref, meta_ref, cnt_ref, carry_s):
    i = pl.program_id(0)
    tm = x_ref.shape[0]

    @pl.when(i == 0)
    def _():
        carry_s[...] = cin_ref[...]

    h1 = (x_ref[...]
          + jnp.dot(pool_ref[...].astype(BF16), wo_ref[0:POOL_WIDTH, :], preferred_element_type=F32)
          + jnp.dot(attn_ref[...].astype(BF16), wo_ref[POOL_WIDTH:, :], preferred_element_type=F32))
    h_ref[...] = h1
    xn = _rms(h1, g_ref[...])
    for j in range(ROW_TILES):
        xe_ref[pl.ds(j, tm, stride=XE_PITCH), :] = xn[:, j * LANES:(j + 1) * LANES]
    logits = jnp.dot(xn.astype(BF16), wr_ref[...], preferred_element_type=F32) + br_ref[...]

    lane = lax.broadcasted_iota(I32, (tm, LANES), 1).astype(F32)
    rmax = lambda v: jnp.max(v, axis=1, keepdims=True)
    rmin = lambda v: jnp.min(v, axis=1, keepdims=True)
    rsum = lambda v: jnp.sum(v, axis=1, keepdims=True)

    is_g = lane < N_GROUPS
    gl = jnp.where(is_g, logits, -jnp.inf)
    gmax = rmax(gl)
    g_sel = rmin(jnp.where(gl == gmax, lane, float(LANES)))
    g_w = 1.0 / rsum(jnp.where(is_g, jnp.exp(gl - gmax), 0.0))
    e0 = N_GROUPS + g_sel * EXPERTS_PER_GROUP
    is_e = (lane >= e0) & (lane < e0 + EXPERTS_PER_GROUP)
    el = jnp.where(is_e, logits, -jnp.inf)
    un = jnp.where(is_e, jnp.exp(el - rmax(el)), 0.0)
    prob = jnp.where(is_e, un / rsum(un), -1.0)
    p0 = rmax(prob)
    i0 = rmin(jnp.where(prob == p0, lane, float(LANES)))
    prob1 = jnp.where(lane == i0, -1.0, prob)
    p1 = rmax(prob1)
    i1 = rmin(jnp.where(prob1 == p1, lane, float(LANES)))
    w0 = g_w * p0 / (p0 + p1)
    w1 = g_w * p1 / (p0 + p1)
    lo = jnp.minimum(i0, i1) - e0
    hi = jnp.maximum(i0, i1) - e0
    cls = g_sel * PAIRS_PER_GROUP + lo * (2 * EXPERTS_PER_GROUP - 1 - lo) * 0.5 + hi - lo - 1.0
    w_a = jnp.where(i0 < i1, w0, w1)
    w_b = jnp.where(i0 < i1, w1, w0)
    xe_ref[pl.ds(ROW_TILES, tm, stride=XE_PITCH), :] = jnp.where(lane == 0.0, w_a, jnp.where(lane == 1.0, w_b, 0.0))

    onehot = lane == cls
    tri = (lax.broadcasted_iota(I32, (tm, tm), 1) < lax.broadcasted_iota(I32, (tm, tm), 0)).astype(BF16)
    before = jnp.dot(tri, onehot.astype(BF16), preferred_element_type=F32) + carry_s[...]
    rank = rsum(jnp.where(onehot, before, 0.0))
    carry_s[...] = carry_s[...] + jnp.sum(onehot.astype(F32), axis=0, keepdims=True)
    cnt_ref[...] = carry_s[...]
    meta = jnp.where(lane == 0.0, cls, jnp.where(lane == 1.0, rank, 0.0))
    meta_ref[...] = jnp.transpose(meta)[0:8, :].astype(I32)


def _outproj_router(pool_o, attn_o, x, wo, g, wr, br, counts_in):
    n = x.shape[0]
    tm = _tile(n, 512)
    row = lambda width: pl.BlockSpec((tm, width), lambda i: (i, 0))
    full = lambda shape: pl.BlockSpec(shape, lambda i: (0,) * len(shape))
    return pl.pallas_call(
        _outproj_router_kernel,
        grid=(n // tm,),
        in_specs=[row(POOL_WIDTH), row(ATTN_WIDTH), row(D_MODEL), full((D_MODEL, D_MODEL)), full((1, D_MODEL)),
                  full((D_MODEL, LANES)), full((1, LANES)), full((1, LANES))],
        out_specs=[row(D_MODEL), pl.BlockSpec((tm * XE_PITCH, LANES), lambda i: (i, 0)),
                   pl.BlockSpec((8, tm), lambda i: (0, i)), full((1, LANES))],
        out_shape=[jax.ShapeDtypeStruct((n, D_MODEL), F32), jax.ShapeDtypeStruct((n * XE_PITCH, LANES), F32),
                   jax.ShapeDtypeStruct((8, n), I32), jax.ShapeDtypeStruct((1, LANES), F32)],
        scratch_shapes=[pltpu.VMEM((1, LANES), F32)],
        compiler_params=_params(("arbitrary",)),
    )(pool_o, attn_o, x, wo, g, wr, br, counts_in)


def _load_dest(ps_ref, cls_hbm, rank_hbm, cls_s, rank_s, sem, base, tm):
    c1 = pltpu.make_async_copy(cls_hbm.at[pl.ds(base, tm)], cls_s, sem.at[0])
    c2 = pltpu.make_async_copy(rank_hbm.at[pl.ds(base, tm)], rank_s, sem.at[1])
    c1.start()
    c2.start()
    c1.wait()
    c2.wait()
    return lambda r: ps_ref[cls_s[r]] + rank_s[r]


def _row_dmas(tm, make_copy, dest):
    def start(r, carry):
        make_copy(r, dest(r)).start()
        return carry

    def wait(r, carry):
        make_copy(r, 0).wait()
        return carry

    lax.fori_loop(0, tm, start, 0, unroll=8)
    lax.fori_loop(0, tm, wait, 0, unroll=8)


def _dispatch_kernel(ps_ref, cls_hbm, rank_hbm, xe_ref, xs_in_hbm, xs_hbm, cls_s, rank_s, sem, *, tm):
    del xs_in_hbm
    base = pl.program_id(0) * tm
    dest = _load_dest(ps_ref, cls_hbm, rank_hbm, cls_s, rank_s, sem, base, tm)
    _row_dmas(tm, lambda r, d: pltpu.make_async_copy(xe_ref.at[pl.ds(r * XE_PITCH, XE_PITCH)],
                                                     xs_hbm.at[pl.ds(d * XE_PITCH, XE_PITCH)], sem.at[2]), dest)


def _dispatch(pstart, cls, rank, xe, xs):
    n = xe.shape[0] // XE_PITCH
    tm = _tile(n, 512)
    any_spec = pl.BlockSpec(memory_space=pl.ANY)
    return pl.pallas_call(
        functools.partial(_dispatch_kernel, tm=tm),
        grid_spec=pltpu.PrefetchScalarGridSpec(
            num_scalar_prefetch=1, grid=(n // tm,),
            in_specs=[any_spec, any_spec, pl.BlockSpec((tm * XE_PITCH, LANES), lambda i, ps: (i, 0)), any_spec],
            out_specs=any_spec,
            scratch_shapes=[pltpu.SMEM((tm,), I32), pltpu.SMEM((tm,), I32), pltpu.SemaphoreType.DMA((3,))]),
        out_shape=jax.ShapeDtypeStruct(xs.shape, xs.dtype),
        input_output_aliases={4: 0},
        compiler_params=pltpu.CompilerParams(dimension_semantics=("arbitrary",), has_side_effects=True,
                                             vmem_limit_bytes=VMEM_LIMIT),
    )(pstart, cls, rank, xe, xs)


def _ffn_kernel(ba_ref, bb_ref, nu_ref, xs_ref, wga_ref, wua_ref, wda_ref, wgb_ref, wub_ref, wdb_ref, ys_ref):
    j = pl.program_id(0)
    rows = MOE_ROWS

    @pl.when(j < nu_ref[0])
    def _():
        x = jnp.concatenate([xs_ref[pl.ds(t, rows, stride=XE_PITCH), :] for t in range(ROW_TILES)], axis=1).astype(BF16)
        wts = xs_ref[pl.ds(ROW_TILES, rows, stride=XE_PITCH), :]

        def expert(wg_ref, wu_ref, wd_ref):
            gate = jnp.dot(x, wg_ref[0], preferred_element_type=F32)
            up = jnp.dot(x, wu_ref[0], preferred_element_type=F32)
            hid = (gate / (1.0 + jnp.exp(-gate))) * up
            return jnp.dot(hid.astype(BF16), wd_ref[0], preferred_element_type=F32)

        y = wts[:, 0:1] * expert(wga_ref, wua_ref, wda_ref) + wts[:, 1:2] * expert(wgb_ref, wub_ref, wdb_ref)
        for t in range(ROW_TILES):
            ys_ref[pl.ds(t, rows, stride=ROW_TILES), :] = y[:, t * LANES:(t + 1) * LANES]

    @pl.when(j >= nu_ref[0])
    def _():
        ys_ref[...] = jnp.zeros(ys_ref.shape, F32)


def _ffn(blk_a, blk_b, n_used, xs, wg, wu, wd):
    n_blk = xs.shape[0] // (MOE_ROWS * XE_PITCH)
    up_a = pl.BlockSpec((1, D_MODEL, EXPERT_HIDDEN), lambda j, a, b, nu: (a[j], 0, 0))
    up_b = pl.BlockSpec((1, D_MODEL, EXPERT_HIDDEN), lambda j, a, b, nu: (b[j], 0, 0))
    dn_a = pl.BlockSpec((1, EXPERT_HIDDEN, D_MODEL), lambda j, a, b, nu: (a[j], 0, 0))
    dn_b = pl.BlockSpec((1, EXPERT_HIDDEN, D_MODEL), lambda j, a, b, nu: (b[j], 0, 0))
    return pl.pallas_call(
        _ffn_kernel,
        grid_spec=pltpu.PrefetchScalarGridSpec(
            num_scalar_prefetch=3, grid=(n_blk,),
            in_specs=[pl.BlockSpec((MOE_ROWS * XE_PITCH, LANES), lambda j, a, b, nu: (j, 0)),
                      up_a, up_a, dn_a, up_b, up_b, dn_b],
            out_specs=pl.BlockSpec((MOE_ROWS * ROW_TILES, LANES), lambda j, a, b, nu: (j, 0))),
        out_shape=jax.ShapeDtypeStruct((n_blk * MOE_ROWS * ROW_TILES, LANES), F32),
        compiler_params=_params(("arbitrary",)),
    )(blk_a, blk_b, n_used, xs, wg, wu, wd, wg, wu, wd)


def _combine_kernel(ps_ref, cls_hbm, rank_hbm, ys_hbm, h_ref, g_ref, y_ref, cls_s, rank_s, ybuf, sem, *, tm):
    base = pl.program_id(0) * tm
    dest = _load_dest(ps_ref, cls_hbm, rank_hbm, cls_s, rank_s, sem, base, tm)
    _row_dmas(tm, lambda r, d: pltpu.make_async_copy(ys_hbm.at[pl.ds(d * ROW_TILES, ROW_TILES)],
                                                     ybuf.at[pl.ds(r * ROW_TILES, ROW_TILES)], sem.at[2]), dest)
    y = jnp.concatenate([ybuf[pl.ds(t, tm, stride=ROW_TILES), :] for t in range(ROW_TILES)], axis=1)
    y_ref[...] = _rms(h_ref[...] + y, g_ref[...])


def _combine(pstart, cls, rank, ys, h1, g):
    n = h1.shape[0]
    tm = _tile(n, 512)
    any_spec = pl.BlockSpec(memory_space=pl.ANY)
    return pl.pallas_call(
        functools.partial(_combine_kernel, tm=tm),
        grid_spec=pltpu.PrefetchScalarGridSpec(
            num_scalar_prefetch=1, grid=(n // tm,),
            in_specs=[any_spec, any_spec, any_spec, pl.BlockSpec((tm, D_MODEL), lambda i, ps: (i, 0)),
                      pl.BlockSpec((1, D_MODEL), lambda i, ps: (0, 0))],
            out_specs=pl.BlockSpec((tm, D_MODEL), lambda i, ps: (i, 0)),
            scratch_shapes=[pltpu.SMEM((tm,), I32), pltpu.SMEM((tm,), I32), pltpu.VMEM((tm * ROW_TILES, LANES), F32),
                            pltpu.SemaphoreType.DMA((3,))]),
        out_shape=jax.ShapeDtypeStruct((n, D_MODEL), F32),
        compiler_params=_params(("arbitrary",)),
    )(pstart, cls, rank, ys, h1, g)


def _moe_and_final(pool_o, attn_o, x, wo, g_ffn, wr, br, wg, wu, wd, g_final):
    n = x.shape[0]
    h1, xe, meta, counts = _outproj_router(pool_o, attn_o, x, wo, g_ffn, wr, br, jnp.zeros((1, LANES), F32))
    cnt = counts[0].astype(I32)
    padded = ((cnt + MOE_ROWS - 1) // MOE_ROWS) * MOE_ROWS
    pends = jnp.cumsum(padded)
    pstart = (pends - padded).astype(I32)
    n_blk = -(-n // MOE_ROWS) + N_CLASSES
    n_used = (pends[-1] // MOE_ROWS).astype(I32)
    blk = jnp.arange(n_blk, dtype=I32)
    blk_last = jnp.minimum(blk, jnp.maximum(n_used - 1, 0))
    blk_cls = jnp.sum((pends[None, :N_CLASSES] <= (blk_last * MOE_ROWS)[:, None]).astype(I32), axis=1)
    blk_cls = jnp.minimum(blk_cls, N_CLASSES - 1)
    blk_a = jnp.asarray(CLASS_A)[blk_cls]
    blk_b = jnp.asarray(CLASS_B)[blk_cls]
    cls, rank = meta[0], meta[1]
    xs = _dispatch(pstart, cls, rank, xe, jnp.zeros((n_blk * MOE_ROWS * XE_PITCH, LANES), F32))
    ys = _ffn(blk_a, blk_b, n_used.reshape(1), xs, wg, wu, wd)
    return _combine(pstart, cls, rank, ys, h1, g_final)


def kernel(x_prompt, x_sample, cache_k, cache_v, cache_idx_k, state_pool, page_table, norm_mix_g, w_in, pool_w,
           pool_scale, w_out, norm_ffn_g, router_group_w, router_group_b, router_expert_w, router_expert_b,
           expert_w_gate, expert_w_up, expert_w_down, final_norm_g):
    depth = w_in.shape[0]
    assert depth == 1, "single-layer step"
    b, s, d = x_prompt.shape
    db, t_new, _ = x_sample.shape
    n_pages = page_table.shape[1]
    past = n_pages * PAGE_SIZE
    n_pool = cache_k.shape[1]
    l = 0

    w_in_b = jnp.pad(w_in[l], ((0, 0), (0, IN_PAD - IN_WIDTH))).astype(BF16)
    g_mix = norm_mix_g[l].reshape(1, d)
    pw = pool_w[l].astype(BF16)
    psc = pool_scale[l].reshape(1, POOL_WIDTH)
    wo = w_out[l].astype(BF16)
    g_ffn = norm_ffn_g[l].reshape(1, d)
    wr = jnp.pad(jnp.concatenate([router_group_w[l], router_expert_w[l]], axis=1),
                 ((0, 0), (0, LANES - N_GROUPS - N_EXPERTS))).astype(BF16)
    br = jnp.pad(jnp.concatenate([router_group_b[l], router_expert_b[l]]), (0, LANES - N_GROUPS - N_EXPERTS)).reshape(1, LANES)
    wg = expert_w_gate[l].astype(BF16)
    wu = expert_w_up[l].astype(BF16)
    wd = expert_w_down[l].astype(BF16)
    g_fin = final_norm_g.reshape(1, d)
    moe = lambda pool_o, attn_o, x: _moe_and_final(pool_o, attn_o, x, wo, g_ffn, wr, br, wg, wu, wd, g_fin)

    xp = x_prompt.reshape(b * s, d)
    tab_p = _rope_tables(jnp.arange(s, dtype=I32))
    u, qt, k, v, qit, ki, kw, kb, vt, kwb = _inproj(xp, g_mix, w_in_b, tab_p, seq_len=s)
    r3 = lambda a: a.reshape(b, s, a.shape[-1])
    pool_o = _pool_prompt(r3(u), pw, psc)
    attn_o = _dsa_prompt(qt, qit, r3(kw), r3(kwb), r3(kb), vt, min(TOPK_MAX, s // 4))
    y_prompt = moe(pool_o.reshape(b * s, POOL_WIDTH), attn_o.reshape(b * s, ATTN_WIDTH), xp).reshape(b, s, d)
    k_prompt = jnp.moveaxis(k.reshape(b, N_KV_HEADS, HEAD_DIM, s), -1, 1)[None]
    v_prompt = jnp.moveaxis(v.reshape(b, N_KV_HEADS, HEAD_DIM, s), -1, 1)[None]
    idx_k_prompt = jnp.moveaxis(ki, -1, 1)[None]
    if s >= POOL_STATE:
        pool_prompt = r3(u)[:, s - POOL_STATE:][None]
    else:
        pool_prompt = jnp.concatenate([jnp.zeros((b, POOL_STATE, POOL_WIDTH), F32), r3(u)], axis=1)[:, -POOL_STATE:][None]

    xs_ = x_sample.reshape(db * t_new, d)
    tab_s = _rope_tables(past + (jnp.arange(db * t_new, dtype=I32) % t_new))
    u2, q2, k2, v2, qi2, ki2, kw2, _, _, _ = _inproj(xs_, g_mix, w_in_b, tab_s)
    u2_3 = u2.reshape(db, t_new, POOL_WIDTH)
    pool2 = _pool_sample(jnp.swapaxes(state_pool[l], 0, 1), jnp.swapaxes(u2_3, 0, 1), pw, psc, past)
    pool2 = jnp.swapaxes(pool2, 0, 1).reshape(db * t_new, POOL_WIDTH)
    page_t = lambda c: jnp.moveaxis(c[l], 1, -1).reshape(n_pool, -1, PAGE_SIZE)
    attn2 = _dsa_sample(page_table, q2, qi2, kw2, k2, v2, page_t(cache_idx_k), page_t(cache_k), page_t(cache_v),
                        min(TOPK_MAX, (past + t_new) // 4))
    y_sample = moe(pool2, attn2, xs_).reshape(db, t_new, d)
    k_sample = k2.reshape(1, db, t_new, N_KV_HEADS, HEAD_DIM)
    v_sample = v2.reshape(1, db, t_new, N_KV_HEADS, HEAD_DIM)
    idx_k_sample = ki2.reshape(1, db, t_new, IDX_DIM)
    pool_sample = jnp.concatenate([state_pool[l], u2_3], axis=1)[:, -POOL_STATE:][None]

    return (y_prompt, y_sample, k_prompt, v_prompt, idx_k_prompt, pool_prompt,
            k_sample, v_sample, idx_k_sample, pool_sample)
```

```python
import functools
import math

import numpy as np
import jax
import jax.numpy as jnp
from jax import lax
from jax.experimental import pallas as pl
from jax.experimental.pallas import tpu as pltpu

F32 = jnp.float32
BF16 = jnp.bfloat16
I32 = jnp.int32

D_MODEL = 1024
HEAD_DIM = 64
N_HEADS = 8
N_KV_HEADS = 2
ROPE_DIM = 16
ROPE_THETA = 500000.0
IDX_HEADS = 8
IDX_DIM = 64
TOPK_MAX = 256
PAGE_SIZE = 128
POOL_WIDTH = 512
POOL_WINDOWS = (2, 4, 8, 16)
POOL_GROUP = 128
POOL_STATE = 15
ATTN_WIDTH = 512
N_GROUPS = 4
EXPERTS_PER_GROUP = 8
N_EXPERTS = 32
EXPERT_HIDDEN = 256
NORM_EPS = 1e-6

LANES = 128
IN_WIDTH = 1864
IN_PAD = 1920
C_U, C_Q, C_K, C_V, C_QI, C_KW = 0, 512, 1024, 1152, 1280, 1792
PAIRS_PER_GROUP = EXPERTS_PER_GROUP * (EXPERTS_PER_GROUP - 1) // 2
N_CLASSES = N_GROUPS * PAIRS_PER_GROUP
MOE_ROWS = 128
ROW_TILES = D_MODEL // LANES
XE_PITCH = ROW_TILES + 1
ATT_SUB_HEADS = 4
VT_ROWS = LANES + 16
LOG2E = 1.4426950408889634
NEG = -1e30
INT_MIN = -2 ** 31
INT_MAX = 2 ** 31 - 1
VMEM_LIMIT = 48 * 1024 * 1024


def _class_tables():
    a, b = [], []
    for g in range(N_GROUPS):
        for lo in range(EXPERTS_PER_GROUP):
            for hi in range(lo + 1, EXPERTS_PER_GROUP):
                a.append(g * EXPERTS_PER_GROUP + lo)
                b.append(g * EXPERTS_PER_GROUP + hi)
    return np.array(a, np.int32), np.array(b, np.int32)


CLASS_A, CLASS_B = _class_tables()


def _tile(n, cap):
    t = min(n, cap)
    while n % t:
        t -= 8
    return t


def _params(sem):
    return pltpu.CompilerParams(dimension_semantics=sem, vmem_limit_bytes=VMEM_LIMIT)


def _rms(x, g):
    return x * lax.rsqrt(jnp.mean(x * x, axis=-1, keepdims=True) + NORM_EPS) * g


def _dot_nt(a, b):
    return lax.dot_general(a, b, (((1,), (1,)), ((), ())), preferred_element_type=F32)


def _tree(parts, op):
    parts = list(parts)
    while len(parts) > 1:
        nxt = [op(parts[a], parts[a + 1]) for a in range(0, len(parts) - 1, 2)]
        if len(parts) % 2:
            nxt.append(parts[-1])
        parts = nxt
    return parts[0]


def _fold_rows(x, op):
    return _tree([x[r:r + 8] for r in range(0, x.shape[0], 8)], op)


def _count_rows(mask, n_acc=8):
    accs = [jnp.zeros((8, mask.shape[1]), F32) for _ in range(n_acc)]
    for a, r in enumerate(range(0, mask.shape[0], 8)):
        acc = accs[a % n_acc]
        accs[a % n_acc] = jnp.where(mask[r:r + 8], acc + 1.0, acc)
    return _fold_rows(jnp.concatenate(accs, axis=0), jnp.add)


def _sort_key(score):
    bits = lax.bitcast_convert_type(score, I32)
    return jnp.where(bits < 0, bits ^ jnp.int32(INT_MAX), bits)


def _rope_tables(pos):
    half = ROPE_DIM // 2
    r = pos.shape[0]
    inv = jnp.power(jnp.float32(ROPE_THETA), -jnp.arange(half, dtype=F32) / half)
    ang = pos.astype(F32)[:, None] * inv[None, :]
    cos, sin = jnp.cos(ang), jnp.sin(ang)
    one = lambda n: jnp.ones((r, n), F32)
    zero = lambda n: jnp.zeros((r, n), F32)
    c64 = jnp.concatenate([cos, cos, one(HEAD_DIM - ROPE_DIM)], axis=1)
    s1_64 = jnp.concatenate([-sin, zero(HEAD_DIM - half)], axis=1)
    s2_64 = jnp.concatenate([zero(half), sin, zero(HEAD_DIM - ROPE_DIM)], axis=1)
    ck = jnp.concatenate([c64, jnp.full((r, IDX_HEADS), IDX_HEADS ** -0.5, F32), one(LANES - HEAD_DIM - IDX_HEADS)], axis=1)
    return jnp.concatenate([c64, c64, s1_64, s1_64, s2_64, s2_64, ck, s1_64, zero(64), s2_64, zero(64)], axis=1)


def _inproj_kernel(x_ref, g_ref, w_ref, tab_ref, u_ref, q_ref, k_ref, v_ref, qi_ref, ki_ref, kw_ref,
                   kb_ref, vb_ref, kwb_ref, *, transposed):
    xb = _rms(x_ref[...], g_ref[...]).astype(BF16)
    tm = xb.shape[0]
    lane = lax.broadcasted_iota(I32, (tm, LANES), 1)

    def seg(c0, c1):
        return jnp.dot(xb, w_ref[:, c0:c1], preferred_element_type=F32)

    def rope(z, t0):
        c, s1, s2 = (tab_ref[:, t0 + i * LANES:t0 + (i + 1) * LANES] for i in range(3))
        return z * c + pltpu.roll(z, LANES - ROPE_DIM // 2, 1) * s1 + pltpu.roll(z, ROPE_DIM // 2, 1) * s2

    u_ref[...] = seg(C_U, C_Q)

    for c_base, out_ref, to_group in ((C_Q, q_ref, True), (C_QI, qi_ref, False)):
        z = seg(c_base, c_base + N_HEADS * HEAD_DIM)
        scale = HEAD_DIM ** -0.5 * (LOG2E if (transposed and to_group) else 1.0)
        for p in range(N_HEADS // 2):
            zz = rope(z[:, p * LANES:(p + 1) * LANES], 0) * scale
            zr = pltpu.roll(zz, HEAD_DIM, 1)
            for hh in range(2):
                h = 2 * p + hh
                tgt = (h // (N_HEADS // N_KV_HEADS)) if to_group else 0
                src = zz if hh == tgt else zr
                keep = (lane >= tgt * HEAD_DIM) & (lane < (tgt + 1) * HEAD_DIM)
                val = jnp.where(keep, src, 0.0)
                if transposed:
                    for qb in range(tm // LANES):
                        blk = jnp.transpose(val[qb * LANES:(qb + 1) * LANES])
                        out_ref[0, qb, :, h * LANES:(h + 1) * LANES] = blk.astype(out_ref.dtype)
                else:
                    out_ref[:, h * LANES:(h + 1) * LANES] = val.astype(out_ref.dtype)

    k = rope(seg(C_K, C_V), 0)
    kb_ref[...] = k.astype(BF16)
    v = seg(C_V, C_QI)
    kw = rope(seg(C_KW, IN_PAD), 3 * LANES)
    kw_ref[...] = kw
    kwb_ref[...] = kw.astype(BF16)
    if transposed:
        for qb in range(tm // LANES):
            cols = slice(qb * LANES, (qb + 1) * LANES)
            k_ref[0, :, cols] = jnp.transpose(k[cols])
            vt = jnp.transpose(v[cols])
            v_ref[0, :, cols] = vt
            vb_ref[0, 0, 0:LANES, cols] = vt.astype(BF16)
            ki_ref[0, :, cols] = jnp.transpose(kw[cols])[:IDX_DIM]
        vb_ref[0, 0, LANES:, :] = jnp.ones((VT_ROWS - LANES, tm), BF16)
    else:
        k_ref[...] = k
        v_ref[...] = v
        vb_ref[...] = v.astype(BF16)
        ki_ref[...] = kw[:, :IDX_DIM]


def _inproj(x, g, w, tab, seq_len=None):
    n = x.shape[0]
    transposed = seq_len is not None
    tm = _tile(seq_len, 512) if transposed else _tile(n, 512)
    n_tab = tab.shape[0] // tm
    row = lambda width: pl.BlockSpec((tm, width), lambda i: (i, 0))
    rows = lambda width, dt: (row(width), jax.ShapeDtypeStruct((n, width), dt))
    if transposed:
        assert tm % LANES == 0
        nb, ns, qpt = n // seq_len, seq_len // tm, tm // LANES
        qt = (pl.BlockSpec((1, qpt, LANES, N_HEADS * LANES), lambda i: (i // ns, i % ns, 0, 0)),
              jax.ShapeDtypeStruct((nb, seq_len // LANES, LANES, N_HEADS * LANES), BF16))
        vt = (pl.BlockSpec((1, 1, VT_ROWS, tm), lambda i: (i // ns, i % ns, 0, 0)),
              jax.ShapeDtypeStruct((nb, ns, VT_ROWS, tm), BF16))
        tr = lambda width: (pl.BlockSpec((1, width, tm), lambda i: (i // ns, 0, i % ns)),
                            jax.ShapeDtypeStruct((nb, width, seq_len), F32))
        kf, vf, kif = tr(LANES), tr(LANES), tr(IDX_DIM)
    else:
        qt = rows(N_HEADS * LANES, F32)
        vt = rows(LANES, BF16)
        kf, vf, kif = rows(LANES, F32), rows(LANES, F32), rows(IDX_DIM, F32)
    outs = [rows(POOL_WIDTH, F32), qt, kf, vf, qt, kif, rows(LANES, F32), rows(LANES, BF16), vt, rows(LANES, BF16)]
    return pl.pallas_call(
        functools.partial(_inproj_kernel, transposed=transposed),
        grid=(n // tm,),
        in_specs=[row(D_MODEL), pl.BlockSpec((1, D_MODEL), lambda i: (0, 0)),
                  pl.BlockSpec((D_MODEL, IN_PAD), lambda i: (0, 0)),
                  pl.BlockSpec((tm, 6 * LANES), lambda i: (i % n_tab, 0))],
        out_specs=[spec for spec, _ in outs],
        out_shape=[shape for _, shape in outs],
        compiler_params=_params(("arbitrary",)),
    )(x, g, w, tab)


def _pool_prompt_kernel(u_ref, pw_ref, sc_ref, o_ref, ext_ref):
    t = pl.program_id(1)
    tp = u_ref.shape[1]
    halo = POOL_STATE + 1

    @pl.when(t == 0)
    def _():
        ext_ref[0:halo, :] = jnp.zeros((halo, POOL_WIDTH), F32)

    @pl.when(t > 0)
    def _():
        ext_ref[0:halo, :] = ext_ref[tp:tp + halo, :]

    ext_ref[halo:halo + tp, :] = u_ref[0]
    pos = t * tp + lax.broadcasted_iota(I32, (tp, 1), 0)
    for g, w in enumerate(POOL_WINDOWS):
        c0 = g * POOL_GROUP
        cur = ext_ref[halo:halo + tp, c0:c0 + POOL_GROUP]
        win = cur
        for j in range(1, w):
            win = win + ext_ref[halo - j:halo - j + tp, c0:c0 + POOL_GROUP]
        cnt = jnp.minimum(pos + 1, w).astype(F32)
        d = (win / cnt - cur).astype(BF16)
        o = jnp.dot(d, pw_ref[g], preferred_element_type=F32) * sc_ref[:, c0:c0 + POOL_GROUP]
        o_ref[0, :, c0:c0 + POOL_GROUP] = o.astype(o_ref.dtype)


def _pool_prompt(u, pw, sc):
    b, s, _ = u.shape
    tp = _tile(s, 512)
    return pl.pallas_call(
        _pool_prompt_kernel,
        grid=(b, s // tp),
        in_specs=[pl.BlockSpec((1, tp, POOL_WIDTH), lambda i, t: (i, t, 0)),
                  pl.BlockSpec((len(POOL_WINDOWS), POOL_GROUP, POOL_GROUP), lambda i, t: (0, 0, 0)),
                  pl.BlockSpec((1, POOL_WIDTH), lambda i, t: (0, 0))],
        out_specs=pl.BlockSpec((1, tp, POOL_WIDTH), lambda i, t: (i, t, 0)),
        out_shape=jax.ShapeDtypeStruct((b, s, POOL_WIDTH), BF16),
        scratch_shapes=[pltpu.VMEM((tp + POOL_STATE + 1, POOL_WIDTH), F32)],
        compiler_params=_params(("arbitrary", "arbitrary")),
    )(u, pw, sc)


def _pool_sample_kernel(st_ref, u_ref, pw_ref, sc_ref, o_ref, *, pos0):
    t_new = u_ref.shape[0]
    ext = [st_ref[i] for i in range(POOL_STATE)] + [u_ref[i] for i in range(t_new)]
    for g, w in enumerate(POOL_WINDOWS):
        c0 = g * POOL_GROUP
        ds = []
        for t in range(t_new):
            e = POOL_STATE + t
            cur = ext[e][:, c0:c0 + POOL_GROUP]
            win = cur
            for j in range(1, w):
                win = win + ext[e - j][:, c0:c0 + POOL_GROUP]
            cnt = float(min(pos0 + t + 1, w))
            ds.append((win / cnt - cur).astype(BF16))
        d = jnp.concatenate(ds, axis=0)
        o = jnp.dot(d, pw_ref[g], preferred_element_type=F32) * sc_ref[:, c0:c0 + POOL_GROUP]
        db = o.shape[0] // t_new
        for t in range(t_new):
            o_ref[t, :, c0:c0 + POOL_GROUP] = o[t * db:(t + 1) * db]


def _pool_sample(st_t, u_t, pw, sc, pos0):
    t_new, db, _ = u_t.shape
    return pl.pallas_call(
        functools.partial(_pool_sample_kernel, pos0=pos0),
        out_shape=jax.ShapeDtypeStruct((t_new, db, POOL_WIDTH), F32),
        compiler_params=pltpu.CompilerParams(vmem_limit_bytes=VMEM_LIMIT),
    )(st_t, u_t, pw, sc)


def _search_threshold(count_ge, shape, topk):
    def bit_body(it, t):
        cand = t + lax.shift_left(jnp.int32(1), 31 - it)
        return jnp.where(count_ge(cand) >= topk, cand, t)
    return lax.fori_loop(0, 32, bit_body, jnp.full(shape, INT_MIN, I32))


def _search_threshold16(count_ge, need, width):
    def bit_body(it, t):
        cand = t + lax.shift_left(jnp.int32(1), 15 - it)
        return jnp.where(count_ge(cand) >= need, cand, t)
    return lax.fori_loop(0, 16, bit_body, jnp.full((1, width), -32768, I32))


def _pack_halves(x):
    half = x.shape[0] // 2
    return (x[:half] & 0xFFFF) | (x[half:] << 16)


def _search_tie_index(count_eq_below, need, shape, n_bits):
    def bit_body(it, j):
        cand = j + lax.shift_left(jnp.int32(1), n_bits - 1 - it)
        return jnp.where(count_eq_below(cand) < need, cand, j)
    return lax.fori_loop(0, n_bits, bit_body, jnp.zeros(shape, I32))


def _place_heads(o_ref_store, heads, lane):
    for p in range(N_HEADS // 2):
        parts = []
        for hh in range(2):
            h = 2 * p + hh
            g = h // (N_HEADS // N_KV_HEADS)
            parts.append(heads[h] if g == hh else pltpu.roll(heads[h], HEAD_DIM, 1))
        o_ref_store(p, jnp.where(lane < HEAD_DIM, parts[0], parts[1]))


def _dsa_prompt_kernel(qt_ref, qit_ref, kwq_ref, kwb_ref, kb_ref, vt_ref, o_ref, key_s, hi_s, lo_s, j_s,
                       *, topk, ck, n_bits):
    i = pl.program_id(1)
    tq = LANES
    nc = (i * tq) // ck + 1
    qpos = i * tq + lax.broadcasted_iota(I32, (ck, tq), 1)
    row = lax.broadcasted_iota(I32, (ck, tq), 0)
    hpg = N_HEADS // N_KV_HEADS

    wit =jnp.transpose(kwq_ref[0])[IDX_DIM:IDX_DIM + IDX_HEADS, :]

    def score_body(c, carry):
        kc = kwb_ref[0, pl.ds(pl.multiple_of(c * ck, ck), ck), :]
        st = jnp.dot(kc, qit_ref[0, 0], preferred_element_type=F32)
        acc = jnp.zeros((ck, tq), F32)
        for h in range(IDX_HEADS):
            acc = acc + wit[h:h + 1, :] * jnp.maximum(st[:, h * tq:(h + 1) * tq], 0.0)
        score = jnp.where(c * ck + row <= qpos, acc + 0.0, -jnp.inf)
        key = _sort_key(score)
        key_s[c] = key
        hi_s[c] = _pack_halves(key >> 16)
        return carry

    lax.fori_loop(0, nc, score_body, 0)

    def count(pred):
        def body(c, acc):
            return acc + _count_rows(pred(key_s[c], c * ck + row))
        acc = lax.fori_loop(0, nc, body, jnp.zeros((8, tq), F32))
        return jnp.sum(acc, axis=0, keepdims=True)

    def count16(packed_s, cand, strict=False):
        c16 = pltpu.bitcast(jnp.broadcast_to(_pack_halves(jnp.concatenate([cand, cand], axis=0)), (8, tq)), jnp.int16)

        def body(c, accs):
            x16 = pltpu.bitcast(packed_s[c], jnp.int16)
            accs = list(accs)
            for a, r in enumerate(range(0, ck, 16)):
                x = x16[r:r + 16]
                hit = (x > c16) if strict else (x >= c16)
                accs[a % 8] = jnp.where(hit, accs[a % 8] + jnp.int16(1), accs[a % 8])
            return tuple(accs)

        accs = lax.fori_loop(0, nc, body, tuple(jnp.zeros((16, tq), jnp.int16) for _ in range(8)))
        w = pltpu.bitcast(_tree(accs, jnp.add), I32)
        return jnp.sum((((w << 16) >> 16) + (w >> 16)).astype(F32), axis=0, keepdims=True)

    t_hi = _search_threshold16(lambda cand: count16(hi_s, cand), float(topk), tq)
    c_above = count16(hi_s, t_hi, strict=True)

    def band_body(c, carry):
        key = key_s[c]
        lo = (key & 0xFFFF) - 32768
        lo_s[c] = _pack_halves(jnp.where((key >> 16) == t_hi, lo, -32768))
        return carry

    lax.fori_loop(0, nc, band_body, 0)
    t_lo = _search_threshold16(lambda cand: count16(lo_s, cand), float(topk) - c_above, tq)
    thr = t_hi * 65536 + (t_lo + 32768)
    c_gt = c_above + count16(lo_s, t_lo, strict=True)
    c_ge = c_above + count16(lo_s, t_lo)
    need = float(topk) - c_gt
    j_s[...] = jnp.full((1, tq), INT_MAX, I32)

    @pl.when(jnp.max(c_ge - c_gt - need) > 0.0)
    def _():
        j_s[...] = _search_tie_index(lambda cand: count(lambda k, idx: (k == thr) & (idx < cand)),
                                     need, (1, tq), n_bits)

    jmax = j_s[...]

    hps = ATT_SUB_HEADS
    sw = hps * tq
    n_sub = N_HEADS // hps

    def att_body(c, carry):
        off = pl.multiple_of(c * ck, ck)
        k = key_s[c]
        idx = c * ck + row
        sel = (k > thr) | ((k == thr) & (idx <= jmax))
        b = jnp.where(sel & (idx <= qpos), 0.0, NEG)
        bias = jnp.concatenate([b] * hps, axis=1)
        kc = kb_ref[0, pl.ds(off, ck), :]
        vc = vt_ref[0, c]
        out = []
        sts = [jnp.dot(kc, qt_ref[0, 0, :, sb * sw:(sb + 1) * sw], preferred_element_type=F32) + bias
               for sb in range(n_sub)]
        ps, ms, alphas = [], [], []
        for sb in range(n_sub):
            m = carry[sb][0]
            m_new = jnp.maximum(m, jnp.max(_fold_rows(sts[sb], jnp.maximum), axis=0, keepdims=True))
            ps.append(jnp.exp2(sts[sb] - m_new).astype(BF16))
            ms.append(m_new)
            alphas.append(jnp.exp2(m - m_new))
        for sb in range(n_sub):
            out.append((ms[sb], alphas[sb] * carry[sb][1] + jnp.dot(vc, ps[sb], preferred_element_type=F32)))
        return tuple(out)

    init = (jnp.full((1, sw), NEG, F32), jnp.zeros((VT_ROWS, sw), F32))
    res = lax.fori_loop(0, nc, att_body, (init,) * n_sub)
    for sb in range(n_sub):
        acc = res[sb][1]
        g = (sb * hps) // hpg
        ot = acc[g * HEAD_DIM:(g + 1) * HEAD_DIM, :] / acc[LANES:LANES + 1, :]
        for r in range(0, hps, 2):
            pair = jnp.concatenate([ot[:, r * tq:(r + 1) * tq], ot[:, (r + 1) * tq:(r + 2) * tq]], axis=0)
            c0 = (sb * hps + r) * HEAD_DIM
            o_ref[0, :, c0:c0 + LANES] = jnp.transpose(pair).astype(o_ref.dtype)


def _dsa_prompt(qt, qit, kw, kwb, kb, vt, topk):
    b, s, _ = kb.shape
    tq = LANES
    ck = vt.shape[-1]
    assert ck >= topk and ck % tq == 0 and s % ck == 0
    n_bits = max(1, int(math.ceil(math.log2(s))) + 1)
    qspec = pl.BlockSpec((1, 1, LANES, N_HEADS * LANES), lambda bi, i: (bi, i, 0, 0))
    kspec = pl.BlockSpec((1, s, LANES), lambda bi, i: (bi, 0, 0))
    return pl.pallas_call(
        functools.partial(_dsa_prompt_kernel, topk=topk, ck=ck, n_bits=n_bits),
        grid=(b, s // tq),
        in_specs=[qspec, qspec, pl.BlockSpec((1, tq, LANES), lambda bi, i: (bi, i, 0)), kspec, kspec,
                  pl.BlockSpec((1, s // ck, VT_ROWS, ck), lambda bi, i: (bi, 0, 0, 0))],
        out_specs=pl.BlockSpec((1, tq, ATTN_WIDTH), lambda bi, i: (bi, i, 0)),
        out_shape=jax.ShapeDtypeStruct((b, s, ATTN_WIDTH), BF16),
        scratch_shapes=[pltpu.VMEM((s // ck, ck, tq), I32), pltpu.VMEM((s // ck, ck // 2, tq), I32),
                        pltpu.VMEM((s // ck, ck // 2, tq), I32), pltpu.VMEM((1, tq), I32)],
        compiler_params=_params(("arbitrary", "arbitrary")),
    )(qt, qit, kw, kwb, kb, vt)


def _dsa_sample_kernel(pt_ref, q_ref, qi_ref, kw_ref, k_ref, v_ref, cik_hbm, ck_hbm, cv_hbm, o_ref,
                       ki_buf, k_buf, v_buf, sems, j_s, *, n_pages, topk, n_bits):
    b = pl.program_id(0)
    nb = pl.num_programs(0)
    slot = b % 2
    t_new = q_ref.shape[0]
    past = n_pages * PAGE_SIZE
    srcs = (cik_hbm, ck_hbm, cv_hbm)
    bufs = (ki_buf, k_buf, v_buf)

    def page_copy(a, page, sl, p):
        return pltpu.make_async_copy(srcs[a].at[page], bufs[a].at[sl, p], sems.at[a, sl])

    def start_all(bb, sl):
        def body(p, carry):
            page = pt_ref[bb * n_pages + p]
            for a in range(3):
                page_copy(a, page, sl, p).start()
            return carry
        lax.fori_loop(0, n_pages, body, 0)

    def wait_all(sl):
        def body(p, carry):
            for a in range(3):
                page_copy(a, 0, sl, p).wait()
            return carry
        lax.fori_loop(0, n_pages, body, 0)

    @pl.when(b == 0)
    def _():
        start_all(0, 0)

    @pl.when(b + 1 < nb)
    def _():
        start_all(b + 1, 1 - slot)

    wait_all(slot)

    pad_rows = LANES - t_new
    tq = lax.broadcasted_iota(I32, (t_new, 1), 0)
    lane_n = lax.broadcasted_iota(I32, (t_new, LANES), 1)
    idx_p = (lax.broadcasted_iota(I32, (n_pages, t_new, PAGE_SIZE), 0) * PAGE_SIZE
             + lax.broadcasted_iota(I32, (n_pages, t_new, PAGE_SIZE), 2))
    rep = lambda x: jnp.concatenate([x] * N_HEADS, axis=x.ndim - 2)
    paged = lambda a: jnp.broadcast_to(a[None], (n_pages,) + a.shape)
    bmm = lambda a, b_, nt: lax.dot_general(a, b_, (((2,), (2 if nt else 1,)), ((0,), (0,))),
                                            preferred_element_type=F32)
    over_pages = lambda x, op: _tree([x[p] for p in range(n_pages)], op)

    qi = jnp.concatenate([qi_ref[:, h * LANES:h * LANES + IDX_DIM] for h in range(IDX_HEADS)], axis=0).astype(BF16)
    wi = kw_ref[:, IDX_DIM:IDX_DIM + IDX_HEADS]
    wcol = jnp.concatenate([wi[:, h:h + 1] for h in range(IDX_HEADS)], axis=0)
    ki_new = jnp.concatenate([kw_ref[:, :IDX_DIM], jnp.zeros((pad_rows, IDX_DIM), F32)], axis=0).astype(BF16)

    def head_sum(r):
        acc = r[..., 0:t_new, :]
        for h in range(1, IDX_HEADS):
            acc = acc + r[..., h * t_new:(h + 1) * t_new, :]
        return acc + 0.0

    score_p = head_sum(wcol * jnp.maximum(bmm(paged(qi), ki_buf[slot].astype(BF16), False), 0.0))
    score_n = jnp.where(lane_n <= tq, head_sum(wcol * jnp.maximum(_dot_nt(qi, ki_new), 0.0)), -jnp.inf)
    key_p = _sort_key(score_p)
    key_n = _sort_key(score_n)
    idx_n = past + lane_n

    def count(pred):
        cp = over_pages(jnp.where(pred(key_p, idx_p), 1.0, 0.0), jnp.add)
        return jnp.sum(cp + jnp.where(pred(key_n, idx_n), 1.0, 0.0), axis=1, keepdims=True)

    thr = _search_threshold(lambda cand: count(lambda k, idx: k >= cand), (t_new, 1), float(topk))
    c_gt = count(lambda k, idx: k > thr)
    c_ge = count(lambda k, idx: k >= thr)
    need = float(topk) - c_gt
    j_s[...] = jnp.full((t_new, 1), INT_MAX, I32)

    @pl.when(jnp.max(c_ge - c_gt - need) > 0.0)
    def _():
        j_s[...] = _search_tie_index(lambda cand: count(lambda k, idx: (k == thr) & (idx < cand)),
                                     need, (t_new, 1), n_bits)

    jmax = j_s[...]
    sel = lambda k, idx: (k > thr) | ((k == thr) & (idx <= jmax))
    bias_p = jnp.where(sel(key_p, idx_p), 0.0, NEG)
    bias_n = jnp.where(sel(key_n, idx_n) & (lane_n <= tq), 0.0, NEG)

    q = jnp.concatenate([q_ref[:, h * LANES:(h + 1) * LANES] for h in range(N_HEADS)], axis=0).astype(BF16)
    zpad = jnp.zeros((pad_rows, LANES), F32)
    k_new = jnp.concatenate([k_ref[...], zpad], axis=0).astype(BF16)
    v_new = jnp.concatenate([v_ref[...], zpad], axis=0).astype(BF16)
    s_p = bmm(paged(q), k_buf[slot].astype(BF16), False) + rep(bias_p)
    s_n = _dot_nt(q, k_new) + rep(bias_n)
    m = jnp.max(jnp.maximum(over_pages(s_p, jnp.maximum), s_n), axis=1, keepdims=True)
    p_p = jnp.exp(s_p - m)
    p_n = jnp.exp(s_n - m)
    l = jnp.sum(over_pages(p_p, jnp.add) + p_n, axis=1, keepdims=True)
    o = (over_pages(bmm(p_p.astype(BF16), v_buf[slot].astype(BF16), True), jnp.add)
         + jnp.dot(p_n.astype(BF16), v_new, preferred_element_type=F32)) / l
    heads = [o[h * t_new:(h + 1) * t_new] for h in range(N_HEADS)]

    def store(p, val):
        o_ref[:, p * LANES:(p + 1) * LANES] = val

    _place_heads(store, heads, lane_n)


def _dsa_sample(page_table, q, qi, kw, k, v, cik, ck, cv, topk):
    db, n_pages = page_table.shape
    t_new = q.shape[0] // db
    past = n_pages * PAGE_SIZE
    n_bits = max(1, int(math.ceil(math.log2(past + LANES))) + 1)
    row = lambda width: pl.BlockSpec((t_new, width), lambda b, pt: (b, 0))
    any_spec = pl.BlockSpec(memory_space=pl.ANY)
    return pl.pallas_call(
        functools.partial(_dsa_sample_kernel, n_pages=n_pages, topk=topk, n_bits=n_bits),
        grid_spec=pltpu.PrefetchScalarGridSpec(
            num_scalar_prefetch=1,
            grid=(db,),
            in_specs=[row(N_HEADS * LANES), row(IDX_HEADS * LANES), row(LANES), row(LANES), row(LANES),
                      any_spec, any_spec, any_spec],
            out_specs=row(ATTN_WIDTH),
            scratch_shapes=[pltpu.VMEM((2, n_pages, IDX_DIM, PAGE_SIZE), F32),
                            pltpu.VMEM((2, n_pages, N_KV_HEADS * HEAD_DIM, PAGE_SIZE), F32),
                            pltpu.VMEM((2, n_pages, N_KV_HEADS * HEAD_DIM, PAGE_SIZE), F32),
                            pltpu.SemaphoreType.DMA((3, 2)),
                            pltpu.VMEM((t_new, 1), I32)]),
        out_shape=jax.ShapeDtypeStruct((db * t_new, ATTN_WIDTH), F32),
        compiler_params=pltpu.CompilerParams(dimension_semantics=("arbitrary",), vmem_limit_bytes=56 * 1024 * 1024),
    )(page_table.reshape(-1), q, qi, kw, k, v, cik, ck, cv)


def _outproj_router_kernel(pool_ref, attn_ref, x_ref, wo_ref, g_ref, wr_ref, br_ref, cin_ref,
                           h_ref, xe_ref, meta_ref, cnt_ref, carry_s):
    i = pl.program_id(0)
    tm = x_ref.shape[0]

    @pl.when(i == 0)
    def _():
        carry_s[...] = cin_ref[...]

    h1 = (x_ref[...]
          + jnp.dot(pool_ref[...].astype(BF16), wo_ref[0:POOL_WIDTH, :], preferred_element_type=F32)
          + jnp.dot(attn_ref[...].astype(BF16), wo_ref[POOL_WIDTH:, :], preferred_element_type=F32))
    h_ref[...] = h1
    xn = _rms(h1, g_ref[...])
    for j in range(ROW_TILES):
        xe_ref[pl.ds(j, tm, stride=XE_PITCH), :] = xn[:, j * LANES:(j + 1) * LANES]
    logits = jnp.dot(xn.astype(BF16), wr_ref[...], preferred_element_type=F32) + br_ref[...]

    lane = lax.broadcasted_iota(I32, (tm, LANES), 1).astype(F32)
    rmax = lambda v: jnp.max(v, axis=1, keepdims=True)
    rmin = lambda v: jnp.min(v, axis=1, keepdims=True)
    rsum = lambda v: jnp.sum(v, axis=1, keepdims=True)

    is_g = lane < N_GROUPS
    gl = jnp.where(is_g, logits, -jnp.inf)
    gmax = rmax(gl)
    g_sel = rmin(jnp.where(gl == gmax, lane, float(LANES)))
    g_w = 1.0 / rsum(jnp.where(is_g, jnp.exp(gl - gmax), 0.0))
    e0 = N_GROUPS + g_sel * EXPERTS_PER_GROUP
    is_e = (lane >= e0) & (lane < e0 + EXPERTS_PER_GROUP)
    el = jnp.where(is_e, logits, -jnp.inf)
    un = jnp.where(is_e, jnp.exp(el - rmax(el)), 0.0)
    prob = jnp.where(is_e, un / rsum(un), -1.0)
    p0 = rmax(prob)
    i0 = rmin(jnp.where(prob == p0, lane, float(LANES)))
    prob1 = jnp.where(lane == i0, -1.0, prob)
    p1 = rmax(prob1)
    i1 = rmin(jnp.where(prob1 == p1, lane, float(LANES)))
    w0 = g_w * p0 / (p0 + p1)
    w1 = g_w * p1 / (p0 + p1)
    lo = jnp.minimum(i0, i1) - e0
    hi = jnp.maximum(i0, i1) - e0
    cls = g_sel * PAIRS_PER_GROUP + lo * (2 * EXPERTS_PER_GROUP - 1 - lo) * 0.5 + hi - lo - 1.0
    w_a = jnp.where(i0 < i1, w0, w1)
    w_b = jnp.where(i0 < i1, w1, w0)
    xe_ref[pl.ds(ROW_TILES, tm, stride=XE_PITCH), :] = jnp.where(lane == 0.0, w_a, jnp.where(lane == 1.0, w_b, 0.0))

    onehot = lane == cls
    tri = (lax.broadcasted_iota(I32, (tm, tm), 1) < lax.broadcasted_iota(I32, (tm, tm), 0)).astype(BF16)
    before = jnp.dot(tri, onehot.astype(BF16), preferred_element_type=F32) + carry_s[...]
    rank = rsum(jnp.where(onehot, before, 0.0))
    carry_s[...] = carry_s[...] + jnp.sum(onehot.astype(F32), axis=0, keepdims=True)
    cnt_ref[...] = carry_s[...]
    meta = jnp.where(lane == 0.0, cls, jnp.where(lane == 1.0, rank, 0.0))
    meta_ref[...] = jnp.transpose(meta)[0:8, :].astype(I32)


def _outproj_router(pool_o, attn_o, x, wo, g, wr, br, counts_in):
    n = x.shape[0]
    tm = _tile(n, 512)
    row = lambda width: pl.BlockSpec((tm, width), lambda i: (i, 0))
    full = lambda shape: pl.BlockSpec(shape, lambda i: (0,) * len(shape))
    return pl.pallas_call(
        _outproj_router_kernel,
        grid=(n // tm,),
        in_specs=[row(POOL_WIDTH), row(ATTN_WIDTH), row(D_MODEL), full((D_MODEL, D_MODEL)), full((1, D_MODEL)),
                  full((D_MODEL, LANES)), full((1, LANES)), full((1, LANES))],
        out_specs=[row(D_MODEL), pl.BlockSpec((tm * XE_PITCH, LANES), lambda i: (i, 0)),
                   pl.BlockSpec((8, tm), lambda i: (0, i)), full((1, LANES))],
        out_shape=[jax.ShapeDtypeStruct((n, D_MODEL), F32), jax.ShapeDtypeStruct((n * XE_PITCH, LANES), F32),
                   jax.ShapeDtypeStruct((8, n), I32), jax.ShapeDtypeStruct((1, LANES), F32)],
        scratch_shapes=[pltpu.VMEM((1, LANES), F32)],
        compiler_params=_params(("arbitrary",)),
    )(pool_o, attn_o, x, wo, g, wr, br, counts_in)


def _load_dest(ps_ref, cls_hbm, rank_hbm, cls_s, rank_s, sem, base, tm):
    c1 = pltpu.make_async_copy(cls_hbm.at[pl.ds(base, tm)], cls_s, sem.at[0])
    c2 = pltpu.make_async_copy(rank_hbm.at[pl.ds(base, tm)], rank_s, sem.at[1])
    c1.start()
    c2.start()
    c1.wait()
    c2.wait()
    return lambda r: ps_ref[cls_s[r]] + rank_s[r]


def _row_dmas(tm, make_copy, dest):
    def start(r, carry):
        make_copy(r, dest(r)).start()
        return carry

    def wait(r, carry):
        make_copy(r, 0).wait()
        return carry

    lax.fori_loop(0, tm, start, 0, unroll=8)
    lax.fori_loop(0, tm, wait, 0, unroll=8)


def _dispatch_kernel(ps_ref, cls_hbm, rank_hbm, xe_ref, xs_in_hbm, xs_hbm, cls_s, rank_s, sem, *, tm):
    del xs_in_hbm
    base = pl.program_id(0) * tm
    dest = _load_dest(ps_ref, cls_hbm, rank_hbm, cls_s, rank_s, sem, base, tm)
    _row_dmas(tm, lambda r, d: pltpu.make_async_copy(xe_ref.at[pl.ds(r * XE_PITCH, XE_PITCH)],
                                                     xs_hbm.at[pl.ds(d * XE_PITCH, XE_PITCH)], sem.at[2]), dest)


def _dispatch(pstart, cls, rank, xe, xs):
    n = xe.shape[0] // XE_PITCH
    tm = _tile(n, 512)
    any_spec = pl.BlockSpec(memory_space=pl.ANY)
    return pl.pallas_call(
        functools.partial(_dispatch_kernel, tm=tm),
        grid_spec=pltpu.PrefetchScalarGridSpec(
            num_scalar_prefetch=1, grid=(n // tm,),
            in_specs=[any_spec, any_spec, pl.BlockSpec((tm * XE_PITCH, LANES), lambda i, ps: (i, 0)), any_spec],
            out_specs=any_spec,
            scratch_shapes=[pltpu.SMEM((tm,), I32), pltpu.SMEM((tm,), I32), pltpu.SemaphoreType.DMA((3,))]),
        out_shape=jax.ShapeDtypeStruct(xs.shape, xs.dtype),
        input_output_aliases={4: 0},
        compiler_params=pltpu.CompilerParams(dimension_semantics=("arbitrary",), has_side_effects=True,
                                             vmem_limit_bytes=VMEM_LIMIT),
    )(pstart, cls, rank, xe, xs)


def _ffn_kernel(ba_ref, bb_ref, nu_ref, xs_ref, wga_ref, wua_ref, wda_ref, wgb_ref, wub_ref, wdb_ref, ys_ref):
    j = pl.program_id(0)
    rows = MOE_ROWS

    @pl.when(j < nu_ref[0])
    def _():
        x = jnp.concatenate([xs_ref[pl.ds(t, rows, stride=XE_PITCH), :] for t in range(ROW_TILES)], axis=1).astype(BF16)
        wts = xs_ref[pl.ds(ROW_TILES, rows, stride=XE_PITCH), :]

        def expert(wg_ref, wu_ref, wd_ref):
            gate = jnp.dot(x, wg_ref[0], preferred_element_type=F32)
            up = jnp.dot(x, wu_ref[0], preferred_element_type=F32)
            hid = (gate / (1.0 + jnp.exp(-gate))) * up
            return jnp.dot(hid.astype(BF16), wd_ref[0], preferred_element_type=F32)

        y = wts[:, 0:1] * expert(wga_ref, wua_ref, wda_ref) + wts[:, 1:2] * expert(wgb_ref, wub_ref, wdb_ref)
        for t in range(ROW_TILES):
            ys_ref[pl.ds(t, rows, stride=ROW_TILES), :] = y[:, t * LANES:(t + 1) * LANES]

    @pl.when(j >= nu_ref[0])
    def _():
        ys_ref[...] = jnp.zeros(ys_ref.shape, F32)


def _ffn(blk_a, blk_b, n_used, xs, wg, wu, wd):
    n_blk = xs.shape[0] // (MOE_ROWS * XE_PITCH)
    up_a = pl.BlockSpec((1, D_MODEL, EXPERT_HIDDEN), lambda j, a, b, nu: (a[j], 0, 0))
    up_b = pl.BlockSpec((1, D_MODEL, EXPERT_HIDDEN), lambda j, a, b, nu: (b[j], 0, 0))
    dn_a = pl.BlockSpec((1, EXPERT_HIDDEN, D_MODEL), lambda j, a, b, nu: (a[j], 0, 0))
    dn_b = pl.BlockSpec((1, EXPERT_HIDDEN, D_MODEL), lambda j, a, b, nu: (b[j], 0, 0))
    return pl.pallas_call(
        _ffn_kernel,
        grid_spec=pltpu.PrefetchScalarGridSpec(
            num_scalar_prefetch=3, grid=(n_blk,),
            in_specs=[pl.BlockSpec((MOE_ROWS * XE_PITCH, LANES), lambda j, a, b, nu: (j, 0)),
                      up_a, up_a, dn_a, up_b, up_b, dn_b],
            out_specs=pl.BlockSpec((MOE_ROWS * ROW_TILES, LANES), lambda j, a, b, nu: (j, 0))),
        out_shape=jax.ShapeDtypeStruct((n_blk * MOE_ROWS * ROW_TILES, LANES), F32),
        compiler_params=_params(("arbitrary",)),
    )(blk_a, blk_b, n_used, xs, wg, wu, wd, wg, wu, wd)


def _combine_kernel(ps_ref, cls_hbm, rank_hbm, ys_hbm, h_ref, g_ref, y_ref, cls_s, rank_s, ybuf, sem, *, tm):
    i = pl.program_id(0)
    n = pl.num_programs(0)

    def idx_copies(t):
        sl = t % 2
        return (pltpu.make_async_copy(cls_hbm.at[pl.ds(t * tm, tm)], cls_s.at[pl.ds(sl * tm, tm)], sem.at[0, sl]),
                pltpu.make_async_copy(rank_hbm.at[pl.ds(t * tm, tm)], rank_s.at[pl.ds(sl * tm, tm)], sem.at[1, sl]))

    def row_copy(sl, r, d):
        return pltpu.make_async_copy(ys_hbm.at[pl.ds(d * ROW_TILES, ROW_TILES)],
                                     ybuf.at[pl.ds((sl * tm + r) * ROW_TILES, ROW_TILES)], sem.at[2, sl])

    def issue_rows(t):
        sl = t % 2

        def body(r, carry):
            row_copy(sl, r, ps_ref[cls_s[sl * tm + r]] + rank_s[sl * tm + r]).start()
            return carry
        lax.fori_loop(0, tm, body, 0, unroll=8)

    @pl.when(i == 0)
    def _():
        for cp in idx_copies(0):
            cp.start()
        for cp in idx_copies(0):
            cp.wait()
        issue_rows(0)

        @pl.when(n > 1)
        def _():
            for cp in idx_copies(1):
                cp.start()

    @pl.when(i + 1 < n)
    def _():
        for cp in idx_copies(i + 1):
            cp.wait()
        issue_rows(i + 1)

    @pl.when(i + 2 < n)
    def _():
        for cp in idx_copies(i + 2):
            cp.start()

    slot = i % 2

    def wait_body(r, carry):
        row_copy(slot, r, 0).wait()
        return carry
    lax.fori_loop(0, tm, wait_body, 0, unroll=8)
    base = slot * tm * ROW_TILES
    y = jnp.concatenate([ybuf[pl.ds(base + t, tm, stride=ROW_TILES), :] for t in range(ROW_TILES)], axis=1)
    y_ref[...] = _rms(h_ref[...] + y, g_ref[...])


def _combine(pstart, cls, rank, ys, h1, g):
    n = h1.shape[0]
    tm = _tile(n, 512)
    any_spec = pl.BlockSpec(memory_space=pl.ANY)
    return pl.pallas_call(
        functools.partial(_combine_kernel, tm=tm),
        grid_spec=pltpu.PrefetchScalarGridSpec(
            num_scalar_prefetch=1, grid=(n // tm,),
            in_specs=[any_spec, any_spec, any_spec, pl.BlockSpec((tm, D_MODEL), lambda i, ps: (i, 0)),
                      pl.BlockSpec((1, D_MODEL), lambda i, ps: (0, 0))],
            out_specs=pl.BlockSpec((tm, D_MODEL), lambda i, ps: (i, 0)),
            scratch_shapes=[pltpu.SMEM((2 * tm,), I32), pltpu.SMEM((2 * tm,), I32),
                            pltpu.VMEM((2 * tm * ROW_TILES, LANES), F32), pltpu.SemaphoreType.DMA((3, 2))]),
        out_shape=jax.ShapeDtypeStruct((n, D_MODEL), F32),
        compiler_params=_params(("arbitrary",)),
    )(pstart, cls, rank, ys, h1, g)


def _moe_and_final(pool_o, attn_o, x, wo, g_ffn, wr, br, wg, wu, wd, g_final):
    n = x.shape[0]
    h1, xe, meta, counts = _outproj_router(pool_o, attn_o, x, wo, g_ffn, wr, br, jnp.zeros((1, LANES), F32))
    cnt = counts[0].astype(I32)
    padded = ((cnt + MOE_ROWS - 1) // MOE_ROWS) * MOE_ROWS
    pends = jnp.cumsum(padded)
    pstart = (pends - padded).astype(I32)
    n_blk = -(-n // MOE_ROWS) + N_CLASSES
    n_used = (pends[-1] // MOE_ROWS).astype(I32)
    blk = jnp.arange(n_blk, dtype=I32)
    blk_last = jnp.minimum(blk, jnp.maximum(n_used - 1, 0))
    blk_cls = jnp.sum((pends[None, :N_CLASSES] <= (blk_last * MOE_ROWS)[:, None]).astype(I32), axis=1)
    blk_cls = jnp.minimum(blk_cls, N_CLASSES - 1)
    blk_a = jnp.asarray(CLASS_A)[blk_cls]
    blk_b = jnp.asarray(CLASS_B)[blk_cls]
    cls, rank = meta[0], meta[1]
    xs = _dispatch(pstart, cls, rank, xe, jnp.zeros((n_blk * MOE_ROWS * XE_PITCH, LANES), F32))
    ys = _ffn(blk_a, blk_b, n_used.reshape(1), xs, wg, wu, wd)
    return _combine(pstart, cls, rank, ys, h1, g_final)


def kernel(x_prompt, x_sample, cache_k, cache_v, cache_idx_k, state_pool, page_table, norm_mix_g, w_in, pool_w,
           pool_scale, w_out, norm_ffn_g, router_group_w, router_group_b, router_expert_w, router_expert_b,
           expert_w_gate, expert_w_up, expert_w_down, final_norm_g):
    depth = w_in.shape[0]
    assert depth == 1, "single-layer step"
    b, s, d = x_prompt.shape
    db, t_new, _ = x_sample.shape
    n_pages = page_table.shape[1]
    past = n_pages * PAGE_SIZE
    n_pool = cache_k.shape[1]
    l = 0

    w_in_b = jnp.pad(w_in[l], ((0, 0), (0, IN_PAD - IN_WIDTH))).astype(BF16)
    g_mix = norm_mix_g[l].reshape(1, d)
    pw = pool_w[l].astype(BF16)
    psc = pool_scale[l].reshape(1, POOL_WIDTH)
    wo = w_out[l].astype(BF16)
    g_ffn = norm_ffn_g[l].reshape(1, d)
    wr = jnp.pad(jnp.concatenate([router_group_w[l], router_expert_w[l]], axis=1),
                 ((0, 0), (0, LANES - N_GROUPS - N_EXPERTS))).astype(BF16)
    br = jnp.pad(jnp.concatenate([router_group_b[l], router_expert_b[l]]), (0, LANES - N_GROUPS - N_EXPERTS)).reshape(1, LANES)
    wg = expert_w_gate[l].astype(BF16)
    wu = expert_w_up[l].astype(BF16)
    wd = expert_w_down[l].astype(BF16)
    g_fin = final_norm_g.reshape(1, d)
    moe = lambda pool_o, attn_o, x: _moe_and_final(pool_o, attn_o, x, wo, g_ffn, wr, br, wg, wu, wd, g_fin)

    xp = x_prompt.reshape(b * s, d)
    tab_p = _rope_tables(jnp.arange(s, dtype=I32))
    u, qt, k, v, qit, ki, kw, kb, vt, kwb = _inproj(xp, g_mix, w_in_b, tab_p, seq_len=s)
    r3 = lambda a: a.reshape(b, s, a.shape[-1])
    pool_o = _pool_prompt(r3(u), pw, psc)
    attn_o = _dsa_prompt(qt, qit, r3(kw), r3(kwb), r3(kb), vt, min(TOPK_MAX, s // 4))
    y_prompt = moe(pool_o.reshape(b * s, POOL_WIDTH), attn_o.reshape(b * s, ATTN_WIDTH), xp).reshape(b, s, d)
    k_prompt = jnp.moveaxis(k.reshape(b, N_KV_HEADS, HEAD_DIM, s), -1, 1)[None]
    v_prompt = jnp.moveaxis(v.reshape(b, N_KV_HEADS, HEAD_DIM, s), -1, 1)[None]
    idx_k_prompt = jnp.moveaxis(ki, -1, 1)[None]
    if s >= POOL_STATE:
        pool_prompt = r3(u)[:, s - POOL_STATE:][None]
    else:
        pool_prompt = jnp.concatenate([jnp.zeros((b, POOL_STATE, POOL_WIDTH), F32), r3(u)], axis=1)[:, -POOL_STATE:][None]

    xs_ = x_sample.reshape(db * t_new, d)
    tab_s = _rope_tables(past + (jnp.arange(db * t_new, dtype=I32) % t_new))
    u2, q2, k2, v2, qi2, ki2, kw2, _, _, _ = _inproj(xs_, g_mix, w_in_b, tab_s)
    u2_3 = u2.reshape(db, t_new, POOL_WIDTH)
    pool2 = _pool_sample(jnp.swapaxes(state_pool[l], 0, 1), jnp.swapaxes(u2_3, 0, 1), pw, psc, past)
    pool2 = jnp.swapaxes(pool2, 0, 1).reshape(db * t_new, POOL_WIDTH)
    page_t = lambda c: jnp.moveaxis(c[l], 1, -1).reshape(n_pool, -1, PAGE_SIZE)
    attn2 = _dsa_sample(page_table, q2, qi2, kw2, k2, v2, page_t(cache_idx_k), page_t(cache_k), page_t(cache_v),
                        min(TOPK_MAX, (past + t_new) // 4))
    y_sample = moe(pool2, attn2, xs_).reshape(db, t_new, d)
    k_sample = k2.reshape(1, db, t_new, N_KV_HEADS, HEAD_DIM)
    v_sample = v2.reshape(1, db, t_new, N_KV_HEADS, HEAD_DIM)
    idx_k_sample = ki2.reshape(1, db, t_new, IDX_DIM)
    pool_sample = jnp.concatenate([state_pool[l], u2_3], axis=1)[:, -POOL_STATE:][None]

    return (y_prompt, y_sample, k_prompt, v_prompt, idx_k_prompt, pool_prompt,
            k_sample, v_sample, idx_k_sample, pool_sample)
```

```python
import functools
import math

import numpy as np
import jax
import jax.numpy as jnp
from jax import lax
from jax.experimental import pallas as pl
from jax.experimental.pallas import tpu as pltpu

F32 = jnp.float32
BF16 = jnp.bfloat16
I32 = jnp.int32

D_MODEL = 1024
HEAD_DIM = 64
N_HEADS = 8
N_KV_HEADS = 2
ROPE_DIM = 16
ROPE_THETA = 500000.0
IDX_HEADS = 8
IDX_DIM = 64
TOPK_MAX = 256
PAGE_SIZE = 128
POOL_WIDTH = 512
POOL_WINDOWS = (2, 4, 8, 16)
POOL_GROUP = 128
POOL_STATE = 15
ATTN_WIDTH = 512
N_GROUPS = 4
EXPERTS_PER_GROUP = 8
N_EXPERTS = 32
EXPERT_HIDDEN = 256
NORM_EPS = 1e-6

LANES = 128
IN_WIDTH = 1864
IN_PAD = 1920
C_U, C_Q, C_K, C_V, C_QI, C_KW = 0, 512, 1024, 1152, 1280, 1792
PAIRS_PER_GROUP = EXPERTS_PER_GROUP * (EXPERTS_PER_GROUP - 1) // 2
N_CLASSES = N_GROUPS * PAIRS_PER_GROUP
MOE_ROWS = 128
ROW_TILES = D_MODEL // LANES
XE_PITCH = ROW_TILES + 1
ATT_SUB_HEADS = 4
VT_ROWS = LANES + 16
LOG2E = 1.4426950408889634
NEG = -1e30
INT_MIN = -2 ** 31
INT_MAX = 2 ** 31 - 1
VMEM_LIMIT = 48 * 1024 * 1024


def _class_tables():
    a, b = [], []
    for g in range(N_GROUPS):
        for lo in range(EXPERTS_PER_GROUP):
            for hi in range(lo + 1, EXPERTS_PER_GROUP):
                a.append(g * EXPERTS_PER_GROUP + lo)
                b.append(g * EXPERTS_PER_GROUP + hi)
    return np.array(a, np.int32), np.array(b, np.int32)


CLASS_A, CLASS_B = _class_tables()


def _tile(n, cap):
    t = min(n, cap)
    while n % t:
        t -= 8
    return t


def _params(sem):
    return pltpu.CompilerParams(dimension_semantics=sem, vmem_limit_bytes=VMEM_LIMIT)


def _rms(x, g):
    return x * lax.rsqrt(jnp.mean(x * x, axis=-1, keepdims=True) + NORM_EPS) * g


def _dot_nt(a, b):
    return lax.dot_general(a, b, (((1,), (1,)), ((), ())), preferred_element_type=F32)


def _tree(parts, op):
    parts = list(parts)
    while len(parts) > 1:
        nxt = [op(parts[a], parts[a + 1]) for a in range(0, len(parts) - 1, 2)]
        if len(parts) % 2:
            nxt.append(parts[-1])
        parts = nxt
    return parts[0]


def _fold_rows(x, op):
    return _tree([x[r:r + 8] for r in range(0, x.shape[0], 8)], op)


def _count_rows(mask, n_acc=8):
    accs = [jnp.zeros((8, mask.shape[1]), F32) for _ in range(n_acc)]
    for a, r in enumerate(range(0, mask.shape[0], 8)):
        acc = accs[a % n_acc]
        accs[a % n_acc] = jnp.where(mask[r:r + 8], acc + 1.0, acc)
    return _fold_rows(jnp.concatenate(accs, axis=0), jnp.add)


def _sort_key(score):
    bits = lax.bitcast_convert_type(score, I32)
    return jnp.where(bits < 0, bits ^ jnp.int32(INT_MAX), bits)


def _rope_tables(pos):
    half = ROPE_DIM // 2
    r = pos.shape[0]
    inv = jnp.power(jnp.float32(ROPE_THETA), -jnp.arange(half, dtype=F32) / half)
    ang = pos.astype(F32)[:, None] * inv[None, :]
    cos, sin = jnp.cos(ang), jnp.sin(ang)
    one = lambda n: jnp.ones((r, n), F32)
    zero = lambda n: jnp.zeros((r, n), F32)
    c64 = jnp.concatenate([cos, cos, one(HEAD_DIM - ROPE_DIM)], axis=1)
    s1_64 = jnp.concatenate([-sin, zero(HEAD_DIM - half)], axis=1)
    s2_64 = jnp.concatenate([zero(half), sin, zero(HEAD_DIM - ROPE_DIM)], axis=1)
    ck = jnp.concatenate([c64, jnp.full((r, IDX_HEADS), IDX_HEADS ** -0.5, F32), one(LANES - HEAD_DIM - IDX_HEADS)], axis=1)
    return jnp.concatenate([c64, c64, s1_64, s1_64, s2_64, s2_64, ck, s1_64, zero(64), s2_64, zero(64)], axis=1)


def _inproj_kernel(x_ref, g_ref, w_ref, tab_ref, u_ref, q_ref, k_ref, v_ref, qi_ref, ki_ref, kw_ref,
                   kb_ref, vb_ref, kwb_ref, *, transposed):
    xb = _rms(x_ref[...], g_ref[...]).astype(BF16)
    tm = xb.shape[0]
    lane = lax.broadcasted_iota(I32, (tm, LANES), 1)

    def seg(c0, c1):
        return jnp.dot(xb, w_ref[:, c0:c1], preferred_element_type=F32)

    def rope(z, t0):
        c, s1, s2 = (tab_ref[:, t0 + i * LANES:t0 + (i + 1) * LANES] for i in range(3))
        return z * c + pltpu.roll(z, LANES - ROPE_DIM // 2, 1) * s1 + pltpu.roll(z, ROPE_DIM // 2, 1) * s2

    u_ref[...] = seg(C_U, C_Q)

    for c_base, out_ref, to_group in ((C_Q, q_ref, True), (C_QI, qi_ref, False)):
        z = seg(c_base, c_base + N_HEADS * HEAD_DIM)
        scale = HEAD_DIM ** -0.5 * (LOG2E if (transposed and to_group) else 1.0)
        for p in range(N_HEADS // 2):
            zz = rope(z[:, p * LANES:(p + 1) * LANES], 0) * scale
            zr = pltpu.roll(zz, HEAD_DIM, 1)
            for hh in range(2):
                h = 2 * p + hh
                tgt = (h // (N_HEADS // N_KV_HEADS)) if to_group else 0
                src = zz if hh == tgt else zr
                keep = (lane >= tgt * HEAD_DIM) & (lane < (tgt + 1) * HEAD_DIM)
                val = jnp.where(keep, src, 0.0)
                if transposed:
                    for qb in range(tm // LANES):
                        blk = jnp.transpose(val[qb * LANES:(qb + 1) * LANES])
                        out_ref[0, qb, :, h * LANES:(h + 1) * LANES] = blk.astype(out_ref.dtype)
                else:
                    out_ref[:, h * LANES:(h + 1) * LANES] = val.astype(out_ref.dtype)

    k = rope(seg(C_K, C_V), 0)
    kb_ref[...] = k.astype(BF16)
    v = seg(C_V, C_QI)
    kw = rope(seg(C_KW, IN_PAD), 3 * LANES)
    kw_ref[...] = kw
    kwb_ref[...] = kw.astype(BF16)
    if transposed:
        for qb in range(tm // LANES):
            cols = slice(qb * LANES, (qb + 1) * LANES)
            k_ref[0, :, cols] = jnp.transpose(k[cols])
            vt = jnp.transpose(v[cols])
            v_ref[0, :, cols] = vt
            vb_ref[0, 0, 0:LANES, cols] = vt.astype(BF16)
            ki_ref[0, :, cols] = jnp.transpose(kw[cols])[:IDX_DIM]
        vb_ref[0, 0, LANES:, :] = jnp.ones((VT_ROWS - LANES, tm), BF16)
    else:
        k_ref[...] = k
        v_ref[...] = v
        vb_ref[...] = v.astype(BF16)
        ki_ref[...] = kw[:, :IDX_DIM]


def _inproj(x, g, w, tab, seq_len=None):
    n = x.shape[0]
    transposed = seq_len is not None
    tm = _tile(seq_len, 512) if transposed else _tile(n, 512)
    n_tab = tab.shape[0] // tm
    row = lambda width: pl.BlockSpec((tm, width), lambda i: (i, 0))
    rows = lambda width, dt: (row(width), jax.ShapeDtypeStruct((n, width), dt))
    if transposed:
        assert tm % LANES == 0
        nb, ns, qpt = n // seq_len, seq_len // tm, tm // LANES
        qt = (pl.BlockSpec((1, qpt, LANES, N_HEADS * LANES), lambda i: (i // ns, i % ns, 0, 0)),
              jax.ShapeDtypeStruct((nb, seq_len // LANES, LANES, N_HEADS * LANES), BF16))
        vt = (pl.BlockSpec((1, 1, VT_ROWS, tm), lambda i: (i // ns, i % ns, 0, 0)),
              jax.ShapeDtypeStruct((nb, ns, VT_ROWS, tm), BF16))
        tr = lambda width: (pl.BlockSpec((1, width, tm), lambda i: (i // ns, 0, i % ns)),
                            jax.ShapeDtypeStruct((nb, width, seq_len), F32))
        kf, vf, kif = tr(LANES), tr(LANES), tr(IDX_DIM)
    else:
        qt = rows(N_HEADS * LANES, F32)
        vt = rows(LANES, BF16)
        kf, vf, kif = rows(LANES, F32), rows(LANES, F32), rows(IDX_DIM, F32)
    outs = [rows(POOL_WIDTH, F32), qt, kf, vf, qt, kif, rows(LANES, F32), rows(LANES, BF16), vt, rows(LANES, BF16)]
    return pl.pallas_call(
        functools.partial(_inproj_kernel, transposed=transposed),
        grid=(n // tm,),
        in_specs=[row(D_MODEL), pl.BlockSpec((1, D_MODEL), lambda i: (0, 0)),
                  pl.BlockSpec((D_MODEL, IN_PAD), lambda i: (0, 0)),
                  pl.BlockSpec((tm, 6 * LANES), lambda i: (i % n_tab, 0))],
        out_specs=[spec for spec, _ in outs],
        out_shape=[shape for _, shape in outs],
        compiler_params=_params(("arbitrary",)),
    )(x, g, w, tab)


def _pool_prompt_kernel(u_ref, pw_ref, sc_ref, o_ref, ext_ref):
    t = pl.program_id(1)
    tp = u_ref.shape[1]
    halo = POOL_STATE + 1

    @pl.when(t == 0)
    def _():
        ext_ref[0:halo, :] = jnp.zeros((halo, POOL_WIDTH), F32)

    @pl.when(t > 0)
    def _():
        ext_ref[0:halo, :] = ext_ref[tp:tp + halo, :]

    ext_ref[halo:halo + tp, :] = u_ref[0]
    pos = t * tp + lax.broadcasted_iota(I32, (tp, 1), 0)
    for g, w in enumerate(POOL_WINDOWS):
        c0 = g * POOL_GROUP
        cur = ext_ref[halo:halo + tp, c0:c0 + POOL_GROUP]
        win = cur
        for j in range(1, w):
            win = win + ext_ref[halo - j:halo - j + tp, c0:c0 + POOL_GROUP]
        cnt = jnp.minimum(pos + 1, w).astype(F32)
        d = (win / cnt - cur).astype(BF16)
        o = jnp.dot(d, pw_ref[g], preferred_element_type=F32) * sc_ref[:, c0:c0 + POOL_GROUP]
        o_ref[0, :, c0:c0 + POOL_GROUP] = o.astype(o_ref.dtype)


def _pool_prompt(u, pw, sc):
    b, s, _ = u.shape
    tp = _tile(s, 512)
    return pl.pallas_call(
        _pool_prompt_kernel,
        grid=(b, s // tp),
        in_specs=[pl.BlockSpec((1, tp, POOL_WIDTH), lambda i, t: (i, t, 0)),
                  pl.BlockSpec((len(POOL_WINDOWS), POOL_GROUP, POOL_GROUP), lambda i, t: (0, 0, 0)),
                  pl.BlockSpec((1, POOL_WIDTH), lambda i, t: (0, 0))],
        out_specs=pl.BlockSpec((1, tp, POOL_WIDTH), lambda i, t: (i, t, 0)),
        out_shape=jax.ShapeDtypeStruct((b, s, POOL_WIDTH), BF16),
        scratch_shapes=[pltpu.VMEM((tp + POOL_STATE + 1, POOL_WIDTH), F32)],
        compiler_params=_params(("arbitrary", "arbitrary")),
    )(u, pw, sc)


def _pool_sample_kernel(st_ref, u_ref, pw_ref, sc_ref, o_ref, *, pos0):
    t_new = u_ref.shape[0]
    ext = [st_ref[i] for i in range(POOL_STATE)] + [u_ref[i] for i in range(t_new)]
    for g, w in enumerate(POOL_WINDOWS):
        c0 = g * POOL_GROUP
        ds = []
        for t in range(t_new):
            e = POOL_STATE + t
            cur = ext[e][:, c0:c0 + POOL_GROUP]
            win = cur
            for j in range(1, w):
                win = win + ext[e - j][:, c0:c0 + POOL_GROUP]
            cnt = float(min(pos0 + t + 1, w))
            ds.append((win / cnt - cur).astype(BF16))
        d = jnp.concatenate(ds, axis=0)
        o = jnp.dot(d, pw_ref[g], preferred_element_type=F32) * sc_ref[:, c0:c0 + POOL_GROUP]
        db = o.shape[0] // t_new
        for t in range(t_new):
            o_ref[t, :, c0:c0 + POOL_GROUP] = o[t * db:(t + 1) * db]


def _pool_sample(st_t, u_t, pw, sc, pos0):
    t_new, db, _ = u_t.shape
    return pl.pallas_call(
        functools.partial(_pool_sample_kernel, pos0=pos0),
        out_shape=jax.ShapeDtypeStruct((t_new, db, POOL_WIDTH), F32),
        compiler_params=pltpu.CompilerParams(vmem_limit_bytes=VMEM_LIMIT),
    )(st_t, u_t, pw, sc)


def _search_threshold(count_ge, shape, topk):
    def bit_body(it, t):
        cand = t + lax.shift_left(jnp.int32(1), 31 - it)
        return jnp.where(count_ge(cand) >= topk, cand, t)
    return lax.fori_loop(0, 32, bit_body, jnp.full(shape, INT_MIN, I32))


def _search_threshold16(count_ge, need, width):
    def bit_body(it, t):
        cand = t + lax.shift_left(jnp.int32(1), 15 - it)
        return jnp.where(count_ge(cand) >= need, cand, t)
    return lax.fori_loop(0, 16, bit_body, jnp.full((1, width), -32768, I32))


def _pack_halves(x):
    half = x.shape[0] // 2
    return (x[:half] & 0xFFFF) | (x[half:] << 16)


def _search_tie_index(count_eq_below, need, shape, n_bits):
    def bit_body(it, j):
        cand = j + lax.shift_left(jnp.int32(1), n_bits - 1 - it)
        return jnp.where(count_eq_below(cand) < need, cand, j)
    return lax.fori_loop(0, n_bits, bit_body, jnp.zeros(shape, I32))


def _place_heads(o_ref_store, heads, lane):
    for p in range(N_HEADS // 2):
        parts = []
        for hh in range(2):
            h = 2 * p + hh
            g = h // (N_HEADS // N_KV_HEADS)
            parts.append(heads[h] if g == hh else pltpu.roll(heads[h], HEAD_DIM, 1))
        o_ref_store(p, jnp.where(lane < HEAD_DIM, parts[0], parts[1]))


def _dsa_prompt_kernel(qt_ref, qit_ref, kwq_ref, kwb_ref, kb_ref, vt_ref, o_ref, key_s, hi_s, lo_s, j_s,
                       *, topk, ck, n_bits):
    i = pl.program_id(1)
    tq = LANES
    nc = (i * tq) // ck + 1
    qpos = i * tq + lax.broadcasted_iota(I32, (ck, tq), 1)
    row = lax.broadcasted_iota(I32, (ck, tq), 0)
    hpg = N_HEADS // N_KV_HEADS

    wit =jnp.transpose(kwq_ref[0])[IDX_DIM:IDX_DIM + IDX_HEADS, :]

    def chunk_loop(body, init):
        pairs = nc // 2
        carry = lax.fori_loop(0, pairs, lambda j, cr: body((2 * j, 2 * j + 1), cr), init)
        return lax.fori_loop(2 * pairs, nc, lambda c, cr: body((c,), cr), carry)

    def score_body(chunks, carry):
        sts = [jnp.dot(kwb_ref[0, pl.ds(pl.multiple_of(c * ck, ck), ck), :], qit_ref[0, 0],
                       preferred_element_type=F32) for c in chunks]
        for c, st in zip(chunks, sts):
            acc = jnp.zeros((ck, tq), F32)
            for h in range(IDX_HEADS):
                acc = acc + wit[h:h + 1, :] * jnp.maximum(st[:, h * tq:(h + 1) * tq], 0.0)
            score = jnp.where(c * ck + row <= qpos, acc + 0.0, -jnp.inf)
            key = _sort_key(score)
            key_s[c] = key
            hi_s[c] = _pack_halves(key >> 16)
        return carry

    chunk_loop(score_body, 0)

    def count(pred):
        def body(c, acc):
            return acc + _count_rows(pred(key_s[c], c * ck + row))
        acc = lax.fori_loop(0, nc, body, jnp.zeros((8, tq), F32))
        return jnp.sum(acc, axis=0, keepdims=True)

    def count16(packed_s, cand, strict=False):
        c16 = pltpu.bitcast(jnp.broadcast_to(_pack_halves(jnp.concatenate([cand, cand], axis=0)), (8, tq)), jnp.int16)

        def body(c, accs):
            x16 = pltpu.bitcast(packed_s[c], jnp.int16)
            accs = list(accs)
            for a, r in enumerate(range(0, ck, 16)):
                x = x16[r:r + 16]
                hit = (x > c16) if strict else (x >= c16)
                accs[a % 8] = jnp.where(hit, accs[a % 8] + jnp.int16(1), accs[a % 8])
            return tuple(accs)

        accs = lax.fori_loop(0, nc, body, tuple(jnp.zeros((16, tq), jnp.int16) for _ in range(8)))
        w = pltpu.bitcast(_tree(accs, jnp.add), I32)
        return jnp.sum((((w << 16) >> 16) + (w >> 16)).astype(F32), axis=0, keepdims=True)

    t_hi = _search_threshold16(lambda cand: count16(hi_s, cand), float(topk), tq)
    c_above = count16(hi_s, t_hi, strict=True)

    def band_body(c, carry):
        key = key_s[c]
        lo = (key & 0xFFFF) - 32768
        lo_s[c] = _pack_halves(jnp.where((key >> 16) == t_hi, lo, -32768))
        return carry

    lax.fori_loop(0, nc, band_body, 0)
    t_lo = _search_threshold16(lambda cand: count16(lo_s, cand), float(topk) - c_above, tq)
    thr = t_hi * 65536 + (t_lo + 32768)
    c_gt = c_above + count16(lo_s, t_lo, strict=True)
    c_ge = c_above + count16(lo_s, t_lo)
    need = float(topk) - c_gt
    j_s[...] = jnp.full((1, tq), INT_MAX, I32)

    @pl.when(jnp.max(c_ge - c_gt - need) > 0.0)
    def _():
        j_s[...] = _search_tie_index(lambda cand: count(lambda k, idx: (k == thr) & (idx < cand)),
                                     need, (1, tq), n_bits)

    jmax = j_s[...]

    hps = ATT_SUB_HEADS
    sw = hps * tq
    n_sub = N_HEADS // hps

    def att_body(chunks, carry):
        sts = []
        for c in chunks:
            k = key_s[c]
            idx = c * ck + row
            sel = (k > thr) | ((k == thr) & (idx <= jmax))
            b = jnp.where(sel & (idx <= qpos), 0.0, NEG)
            bias = jnp.concatenate([b] * hps, axis=1)
            kc = kb_ref[0, pl.ds(pl.multiple_of(c * ck, ck), ck), :]
            sts.append([jnp.dot(kc, qt_ref[0, 0, :, sb * sw:(sb + 1) * sw], preferred_element_type=F32) + bias
                        for sb in range(n_sub)])
        ms = [cr[0] for cr in carry]
        ps, alphas = [], []
        for st_c in sts:
            ps.append([])
            alphas.append([])
            for sb in range(n_sub):
                m_new = jnp.maximum(ms[sb], jnp.max(_fold_rows(st_c[sb], jnp.maximum), axis=0, keepdims=True))
                ps[-1].append(jnp.exp2(st_c[sb] - m_new).astype(BF16))
                alphas[-1].append(jnp.exp2(ms[sb] - m_new))
                ms[sb] = m_new
        accs = [cr[1] for cr in carry]
        for c, p_c, a_c in zip(chunks, ps, alphas):
            vc = vt_ref[0, c]
            for sb in range(n_sub):
                accs[sb] = a_c[sb] * accs[sb] + jnp.dot(vc, p_c[sb], preferred_element_type=F32)
        return tuple((ms[sb], accs[sb]) for sb in range(n_sub))

    init = (jnp.full((1, sw), NEG, F32), jnp.zeros((VT_ROWS, sw), F32))
    res = chunk_loop(att_body, (init,) * n_sub)
    for sb in range(n_sub):
        acc = res[sb][1]
        g = (sb * hps) // hpg
        ot = acc[g * HEAD_DIM:(g + 1) * HEAD_DIM, :] / acc[LANES:LANES + 1, :]
        for r in range(0, hps, 2):
            pair = jnp.concatenate([ot[:, r * tq:(r + 1) * tq], ot[:, (r + 1) * tq:(r + 2) * tq]], axis=0)
            c0 = (sb * hps + r) * HEAD_DIM
            o_ref[0, :, c0:c0 + LANES] = jnp.transpose(pair).astype(o_ref.dtype)


def _dsa_prompt(qt, qit, kw, kwb, kb, vt, topk):
    b, s, _ = kb.shape
    tq = LANES
    ck = vt.shape[-1]
    assert ck >= topk and ck % tq == 0 and s % ck == 0
    n_bits = max(1, int(math.ceil(math.log2(s))) + 1)
    qspec = pl.BlockSpec((1, 1, LANES, N_HEADS * LANES), lambda bi, i: (bi, i, 0, 0))
    kspec = pl.BlockSpec((1, s, LANES), lambda bi, i: (bi, 0, 0))
    return pl.pallas_call(
        functools.partial(_dsa_prompt_kernel, topk=topk, ck=ck, n_bits=n_bits),
        grid=(b, s // tq),
        in_specs=[qspec, qspec, pl.BlockSpec((1, tq, LANES), lambda bi, i: (bi, i, 0)), kspec, kspec,
                  pl.BlockSpec((1, s // ck, VT_ROWS, ck), lambda bi, i: (bi, 0, 0, 0))],
        out_specs=pl.BlockSpec((1, tq, ATTN_WIDTH), lambda bi, i: (bi, i, 0)),
        out_shape=jax.ShapeDtypeStruct((b, s, ATTN_WIDTH), BF16),
        scratch_shapes=[pltpu.VMEM((s // ck, ck, tq), I32), pltpu.VMEM((s // ck, ck // 2, tq), I32),
                        pltpu.VMEM((s // ck, ck // 2, tq), I32), pltpu.VMEM((1, tq), I32)],
        compiler_params=_params(("arbitrary", "arbitrary")),
    )(qt, qit, kw, kwb, kb, vt)


def _dsa_sample_kernel(pt_ref, q_ref, qi_ref, kw_ref, k_ref, v_ref, cik_hbm, ck_hbm, cv_hbm, o_ref,
                       ki_buf, k_buf, v_buf, sems, j_s, *, n_pages, topk, n_bits):
    b = pl.program_id(0)
    nb = pl.num_programs(0)
    slot = b % 2
    t_new = q_ref.shape[0]
    past = n_pages * PAGE_SIZE
    srcs = (cik_hbm, ck_hbm, cv_hbm)
    bufs = (ki_buf, k_buf, v_buf)

    def page_copy(a, page, sl, p):
        return pltpu.make_async_copy(srcs[a].at[page], bufs[a].at[sl, p], sems.at[a, sl])

    def start_all(bb, sl):
        def body(p, carry):
            page = pt_ref[bb * n_pages + p]
            for a in range(3):
                page_copy(a, page, sl, p).start()
            return carry
        lax.fori_loop(0, n_pages, body, 0)

    def wait_all(sl):
        def body(p, carry):
            for a in range(3):
                page_copy(a, 0, sl, p).wait()
            return carry
        lax.fori_loop(0, n_pages, body, 0)

    @pl.when(b == 0)
    def _():
        start_all(0, 0)

    @pl.when(b + 1 < nb)
    def _():
        start_all(b + 1, 1 - slot)

    wait_all(slot)

    pad_rows = LANES - t_new
    tq = lax.broadcasted_iota(I32, (t_new, 1), 0)
    lane_n = lax.broadcasted_iota(I32, (t_new, LANES), 1)
    idx_p = (lax.broadcasted_iota(I32, (n_pages, t_new, PAGE_SIZE), 0) * PAGE_SIZE
             + lax.broadcasted_iota(I32, (n_pages, t_new, PAGE_SIZE), 2))
    rep = lambda x: jnp.concatenate([x] * N_HEADS, axis=x.ndim - 2)
    paged = lambda a: jnp.broadcast_to(a[None], (n_pages,) + a.shape)
    bmm = lambda a, b_, nt: lax.dot_general(a, b_, (((2,), (2 if nt else 1,)), ((0,), (0,))),
                                            preferred_element_type=F32)
    over_pages = lambda x, op: _tree([x[p] for p in range(n_pages)], op)

    qi = jnp.concatenate([qi_ref[:, h * LANES:h * LANES + IDX_DIM] for h in range(IDX_HEADS)], axis=0).astype(BF16)
    wi = kw_ref[:, IDX_DIM:IDX_DIM + IDX_HEADS]
    wcol = jnp.concatenate([wi[:, h:h + 1] for h in range(IDX_HEADS)], axis=0)
    ki_new = jnp.concatenate([kw_ref[:, :IDX_DIM], jnp.zeros((pad_rows, IDX_DIM), F32)], axis=0).astype(BF16)

    def head_sum(r):
        acc = r[..., 0:t_new, :]
        for h in range(1, IDX_HEADS):
            acc = acc + r[..., h * t_new:(h + 1) * t_new, :]
        return acc + 0.0

    score_p = head_sum(wcol * jnp.maximum(bmm(paged(qi), ki_buf[slot].astype(BF16), False), 0.0))
    score_n = jnp.where(lane_n <= tq, head_sum(wcol * jnp.maximum(_dot_nt(qi, ki_new), 0.0)), -jnp.inf)
    key_p = _sort_key(score_p)
    key_n = _sort_key(score_n)
    idx_n = past + lane_n

    def count(pred):
        cp = over_pages(jnp.where(pred(key_p, idx_p), 1.0, 0.0), jnp.add)
        return jnp.sum(cp + jnp.where(pred(key_n, idx_n), 1.0, 0.0), axis=1, keepdims=True)

    thr = _search_threshold(lambda cand: count(lambda k, idx: k >= cand), (t_new, 1), float(topk))
    c_gt = count(lambda k, idx: k > thr)
    c_ge = count(lambda k, idx: k >= thr)
    need = float(topk) - c_gt
    j_s[...] = jnp.full((t_new, 1), INT_MAX, I32)

    @pl.when(jnp.max(c_ge - c_gt - need) > 0.0)
    def _():
        j_s[...] = _search_tie_index(lambda cand: count(lambda k, idx: (k == thr) & (idx < cand)),
                                     need, (t_new, 1), n_bits)

    jmax = j_s[...]
    sel = lambda k, idx: (k > thr) | ((k == thr) & (idx <= jmax))
    bias_p = jnp.where(sel(key_p, idx_p), 0.0, NEG)
    bias_n = jnp.where(sel(key_n, idx_n) & (lane_n <= tq), 0.0, NEG)

    q = jnp.concatenate([q_ref[:, h * LANES:(h + 1) * LANES] for h in range(N_HEADS)], axis=0).astype(BF16)
    zpad = jnp.zeros((pad_rows, LANES), F32)
    k_new = jnp.concatenate([k_ref[...], zpad], axis=0).astype(BF16)
    v_new = jnp.concatenate([v_ref[...], zpad], axis=0).astype(BF16)
    s_p = bmm(paged(q), k_buf[slot].astype(BF16), False) + rep(bias_p)
    s_n = _dot_nt(q, k_new) + rep(bias_n)
    m = jnp.max(jnp.maximum(over_pages(s_p, jnp.maximum), s_n), axis=1, keepdims=True)
    p_p = jnp.exp(s_p - m)
    p_n = jnp.exp(s_n - m)
    l = jnp.sum(over_pages(p_p, jnp.add) + p_n, axis=1, keepdims=True)
    o = (over_pages(bmm(p_p.astype(BF16), v_buf[slot].astype(BF16), True), jnp.add)
         + jnp.dot(p_n.astype(BF16), v_new, preferred_element_type=F32)) / l
    heads = [o[h * t_new:(h + 1) * t_new] for h in range(N_HEADS)]

    def store(p, val):
        o_ref[:, p * LANES:(p + 1) * LANES] = val

    _place_heads(store, heads, lane_n)


def _dsa_sample(page_table, q, qi, kw, k, v, cik, ck, cv, topk):
    db, n_pages = page_table.shape
    t_new = q.shape[0] // db
    past = n_pages * PAGE_SIZE
    n_bits = max(1, int(math.ceil(math.log2(past + LANES))) + 1)
    row = lambda width: pl.BlockSpec((t_new, width), lambda b, pt: (b, 0))
    any_spec = pl.BlockSpec(memory_space=pl.ANY)
    return pl.pallas_call(
        functools.partial(_dsa_sample_kernel, n_pages=n_pages, topk=topk, n_bits=n_bits),
        grid_spec=pltpu.PrefetchScalarGridSpec(
            num_scalar_prefetch=1,
            grid=(db,),
            in_specs=[row(N_HEADS * LANES), row(IDX_HEADS * LANES), row(LANES), row(LANES), row(LANES),
                      any_spec, any_spec, any_spec],
            out_specs=row(ATTN_WIDTH),
            scratch_shapes=[pltpu.VMEM((2, n_pages, IDX_DIM, PAGE_SIZE), F32),
                            pltpu.VMEM((2, n_pages, N_KV_HEADS * HEAD_DIM, PAGE_SIZE), F32),
                            pltpu.VMEM((2, n_pages, N_KV_HEADS * HEAD_DIM, PAGE_SIZE), F32),
                            pltpu.SemaphoreType.DMA((3, 2)),
                            pltpu.VMEM((t_new, 1), I32)]),
        out_shape=jax.ShapeDtypeStruct((db * t_new, ATTN_WIDTH), F32),
        compiler_params=pltpu.CompilerParams(dimension_semantics=("arbitrary",), vmem_limit_bytes=56 * 1024 * 1024),
    )(page_table.reshape(-1), q, qi, kw, k, v, cik, ck, cv)


def _outproj_router_kernel(pool_ref, attn_ref, x_ref, wo_ref, g_ref, wr_ref, br_ref, cin_ref,
                           h_ref, xe_ref, meta_ref, cnt_ref, carry_s):
    i = pl.program_id(0)
    tm = x_ref.shape[0]

    @pl.when(i == 0)
    def _():
        carry_s[...] = cin_ref[...]

    h1 = (x_ref[...]
          + jnp.dot(pool_ref[...].astype(BF16), wo_ref[0:POOL_WIDTH, :], preferred_element_type=F32)
          + jnp.dot(attn_ref[...].astype(BF16), wo_ref[POOL_WIDTH:, :], preferred_element_type=F32))
    h_ref[...] = h1
    xn = _rms(h1, g_ref[...])
    for j in range(ROW_TILES):
        xe_ref[pl.ds(j, tm, stride=XE_PITCH), :] = xn[:, j * LANES:(j + 1) * LANES]
    logits = jnp.dot(xn.astype(BF16), wr_ref[...], preferred_element_type=F32) + br_ref[...]

    lane = lax.broadcasted_iota(I32, (tm, LANES), 1).astype(F32)
    rmax = lambda v: jnp.max(v, axis=1, keepdims=True)
    rmin = lambda v: jnp.min(v, axis=1, keepdims=True)
    rsum = lambda v: jnp.sum(v, axis=1, keepdims=True)

    is_g = lane < N_GROUPS
    gl = jnp.where(is_g, logits, -jnp.inf)
    gmax = rmax(gl)
    g_sel = rmin(jnp.where(gl == gmax, lane, float(LANES)))
    g_w = 1.0 / rsum(jnp.where(is_g, jnp.exp(gl - gmax), 0.0))
    e0 = N_GROUPS + g_sel * EXPERTS_PER_GROUP
    is_e = (lane >= e0) & (lane < e0 + EXPERTS_PER_GROUP)
    el = jnp.where(is_e, logits, -jnp.inf)
    un = jnp.where(is_e, jnp.exp(el - rmax(el)), 0.0)
    prob = jnp.where(is_e, un / rsum(un), -1.0)
    p0 = rmax(prob)
    i0 = rmin(jnp.where(prob == p0, lane, float(LANES)))
    prob1 = jnp.where(lane == i0, -1.0, prob)
    p1 = rmax(prob1)
    i1 = rmin(jnp.where(prob1 == p1, lane, float(LANES)))
    w0 = g_w * p0 / (p0 + p1)
    w1 = g_w * p1 / (p0 + p1)
    lo = jnp.minimum(i0, i1) - e0
    hi = jnp.maximum(i0, i1) - e0
    cls = g_sel * PAIRS_PER_GROUP + lo * (2 * EXPERTS_PER_GROUP - 1 - lo) * 0.5 + hi - lo - 1.0
    w_a = jnp.where(i0 < i1, w0, w1)
    w_b = jnp.where(i0 < i1, w1, w0)
    xe_ref[pl.ds(ROW_TILES, tm, stride=XE_PITCH), :] = jnp.where(lane == 0.0, w_a, jnp.where(lane == 1.0, w_b, 0.0))

    onehot = lane == cls
    tri = (lax.broadcasted_iota(I32, (tm, tm), 1) < lax.broadcasted_iota(I32, (tm, tm), 0)).astype(BF16)
    before = jnp.dot(tri, onehot.astype(BF16), preferred_element_type=F32) + carry_s[...]
    rank = rsum(jnp.where(onehot, before, 0.0))
    carry_s[...] = carry_s[...] + jnp.sum(onehot.astype(F32), axis=0, keepdims=True)
    cnt_ref[...] = carry_s[...]
    meta = jnp.where(lane == 0.0, cls, jnp.where(lane == 1.0, rank, 0.0))
    meta_ref[...] = jnp.transpose(meta)[0:8, :].astype(I32)


def _outproj_router(pool_o, attn_o, x, wo, g, wr, br, counts_in):
    n = x.shape[0]
    tm = _tile(n, 512)
    row = lambda width: pl.BlockSpec((tm, width), lambda i: (i, 0))
    full = lambda shape: pl.BlockSpec(shape, lambda i: (0,) * len(shape))
    return pl.pallas_call(
        _outproj_router_kernel,
        grid=(n // tm,),
        in_specs=[row(POOL_WIDTH), row(ATTN_WIDTH), row(D_MODEL), full((D_MODEL, D_MODEL)), full((1, D_MODEL)),
                  full((D_MODEL, LANES)), full((1, LANES)), full((1, LANES))],
        out_specs=[row(D_MODEL), pl.BlockSpec((tm * XE_PITCH, LANES), lambda i: (i, 0)),
                   pl.BlockSpec((8, tm), lambda i: (0, i)), full((1, LANES))],
        out_shape=[jax.ShapeDtypeStruct((n, D_MODEL), F32), jax.ShapeDtypeStruct((n * XE_PITCH, LANES), F32),
                   jax.ShapeDtypeStruct((8, n), I32), jax.ShapeDtypeStruct((1, LANES), F32)],
        scratch_shapes=[pltpu.VMEM((1, LANES), F32)],
        compiler_params=_params(("arbitrary",)),
    )(pool_o, attn_o, x, wo, g, wr, br, counts_in)


def _load_dest(ps_ref, cls_hbm, rank_hbm, cls_s, rank_s, sem, base, tm):
    c1 = pltpu.make_async_copy(cls_hbm.at[pl.ds(base, tm)], cls_s, sem.at[0])
    c2 = pltpu.make_async_copy(rank_hbm.at[pl.ds(base, tm)], rank_s, sem.at[1])
    c1.start()
    c2.start()
    c1.wait()
    c2.wait()
    return lambda r: ps_ref[cls_s[r]] + rank_s[r]


def _row_dmas(tm, make_copy, dest):
    def start(r, carry):
        make_copy(r, dest(r)).start()
        return carry

    def wait(r, carry):
        make_copy(r, 0).wait()
        return carry

    lax.fori_loop(0, tm, start, 0, unroll=8)
    lax.fori_loop(0, tm, wait, 0, unroll=8)


def _dispatch_kernel(ps_ref, cls_hbm, rank_hbm, xe_ref, xs_in_hbm, xs_hbm, cls_s, rank_s, sem, *, tm):
    del xs_in_hbm
    base = pl.program_id(0) * tm
    dest = _load_dest(ps_ref, cls_hbm, rank_hbm, cls_s, rank_s, sem, base, tm)
    _row_dmas(tm, lambda r, d: pltpu.make_async_copy(xe_ref.at[pl.ds(r * XE_PITCH, XE_PITCH)],
                                                     xs_hbm.at[pl.ds(d * XE_PITCH, XE_PITCH)], sem.at[2]), dest)


def _dispatch(pstart, cls, rank, xe, xs):
    n = xe.shape[0] // XE_PITCH
    tm = _tile(n, 512)
    any_spec = pl.BlockSpec(memory_space=pl.ANY)
    return pl.pallas_call(
        functools.partial(_dispatch_kernel, tm=tm),
        grid_spec=pltpu.PrefetchScalarGridSpec(
            num_scalar_prefetch=1, grid=(n // tm,),
            in_specs=[any_spec, any_spec, pl.BlockSpec((tm * XE_PITCH, LANES), lambda i, ps: (i, 0)), any_spec],
            out_specs=any_spec,
            scratch_shapes=[pltpu.SMEM((tm,), I32), pltpu.SMEM((tm,), I32), pltpu.SemaphoreType.DMA((3,))]),
        out_shape=jax.ShapeDtypeStruct(xs.shape, xs.dtype),
        input_output_aliases={4: 0},
        compiler_params=pltpu.CompilerParams(dimension_semantics=("arbitrary",), has_side_effects=True,
                                             vmem_limit_bytes=VMEM_LIMIT),
    )(pstart, cls, rank, xe, xs)


def _ffn_kernel(ba_ref, bb_ref, nu_ref, xs_ref, wga_ref, wua_ref, wda_ref, wgb_ref, wub_ref, wdb_ref, ys_ref):
    j = pl.program_id(0)
    rows = MOE_ROWS

    @pl.when(j < nu_ref[0])
    def _():
        x = jnp.concatenate([xs_ref[pl.ds(t, rows, stride=XE_PITCH), :] for t in range(ROW_TILES)], axis=1).astype(BF16)
        wts = xs_ref[pl.ds(ROW_TILES, rows, stride=XE_PITCH), :]

        def expert(wg_ref, wu_ref, wd_ref):
            gate = jnp.dot(x, wg_ref[0], preferred_element_type=F32)
            up = jnp.dot(x, wu_ref[0], preferred_element_type=F32)
            hid = (gate / (1.0 + jnp.exp(-gate))) * up
            return jnp.dot(hid.astype(BF16), wd_ref[0], preferred_element_type=F32)

        y = wts[:, 0:1] * expert(wga_ref, wua_ref, wda_ref) + wts[:, 1:2] * expert(wgb_ref, wub_ref, wdb_ref)
        for t in range(ROW_TILES):
            ys_ref[pl.ds(t, rows, stride=ROW_TILES), :] = y[:, t * LANES:(t + 1) * LANES]

    @pl.when(j >= nu_ref[0])
    def _():
        ys_ref[...] = jnp.zeros(ys_ref.shape, F32)


def _ffn(blk_a, blk_b, n_used, xs, wg, wu, wd):
    n_blk = xs.shape[0] // (MOE_ROWS * XE_PITCH)
    up_a = pl.BlockSpec((1, D_MODEL, EXPERT_HIDDEN), lambda j, a, b, nu: (a[j], 0, 0))
    up_b = pl.BlockSpec((1, D_MODEL, EXPERT_HIDDEN), lambda j, a, b, nu: (b[j], 0, 0))
    dn_a = pl.BlockSpec((1, EXPERT_HIDDEN, D_MODEL), lambda j, a, b, nu: (a[j], 0, 0))
    dn_b = pl.BlockSpec((1, EXPERT_HIDDEN, D_MODEL), lambda j, a, b, nu: (b[j], 0, 0))
    return pl.pallas_call(
        _ffn_kernel,
        grid_spec=pltpu.PrefetchScalarGridSpec(
            num_scalar_prefetch=3, grid=(n_blk,),
            in_specs=[pl.BlockSpec((MOE_ROWS * XE_PITCH, LANES), lambda j, a, b, nu: (j, 0)),
                      up_a, up_a, dn_a, up_b, up_b, dn_b],
            out_specs=pl.BlockSpec((MOE_ROWS * ROW_TILES, LANES), lambda j, a, b, nu: (j, 0))),
        out_shape=jax.ShapeDtypeStruct((n_blk * MOE_ROWS * ROW_TILES, LANES), F32),
        compiler_params=_params(("arbitrary",)),
    )(blk_a, blk_b, n_used, xs, wg, wu, wd, wg, wu, wd)


def _combine_kernel(ps_ref, cls_hbm, rank_hbm, ys_hbm, h_ref, g_ref, y_ref, cls_s, rank_s, ybuf, sem, *, tm):
    i = pl.program_id(0)
    n = pl.num_programs(0)

    def idx_copies(t):
        sl = t % 2
        return (pltpu.make_async_copy(cls_hbm.at[pl.ds(t * tm, tm)], cls_s.at[pl.ds(sl * tm, tm)], sem.at[0, sl]),
                pltpu.make_async_copy(rank_hbm.at[pl.ds(t * tm, tm)], rank_s.at[pl.ds(sl * tm, tm)], sem.at[1, sl]))

    def row_copy(sl, r, d):
        return pltpu.make_async_copy(ys_hbm.at[pl.ds(d * ROW_TILES, ROW_TILES)],
                                     ybuf.at[pl.ds((sl * tm + r) * ROW_TILES, ROW_TILES)], sem.at[2, sl])

    def issue_rows(t):
        sl = t % 2

        def body(r, carry):
            row_copy(sl, r, ps_ref[cls_s[sl * tm + r]] + rank_s[sl * tm + r]).start()
            return carry
        lax.fori_loop(0, tm, body, 0, unroll=8)

    @pl.when(i == 0)
    def _():
        for cp in idx_copies(0):
            cp.start()
        for cp in idx_copies(0):
            cp.wait()
        issue_rows(0)

        @pl.when(n > 1)
        def _():
            for cp in idx_copies(1):
                cp.start()

    @pl.when(i + 1 < n)
    def _():
        for cp in idx_copies(i + 1):
            cp.wait()
        issue_rows(i + 1)

    @pl.when(i + 2 < n)
    def _():
        for cp in idx_copies(i + 2):
            cp.start()

    slot = i % 2

    def wait_body(r, carry):
        row_copy(slot, r, 0).wait()
        return carry
    lax.fori_loop(0, tm, wait_body, 0, unroll=8)
    base = slot * tm * ROW_TILES
    y = jnp.concatenate([ybuf[pl.ds(base + t, tm, stride=ROW_TILES), :] for t in range(ROW_TILES)], axis=1)
    y_ref[...] = _rms(h_ref[...] + y, g_ref[...])


def _combine(pstart, cls, rank, ys, h1, g):
    n = h1.shape[0]
    tm = _tile(n, 512)
    any_spec = pl.BlockSpec(memory_space=pl.ANY)
    return pl.pallas_call(
        functools.partial(_combine_kernel, tm=tm),
        grid_spec=pltpu.PrefetchScalarGridSpec(
            num_scalar_prefetch=1, grid=(n // tm,),
            in_specs=[any_spec, any_spec, any_spec, pl.BlockSpec((tm, D_MODEL), lambda i, ps: (i, 0)),
                      pl.BlockSpec((1, D_MODEL), lambda i, ps: (0, 0))],
            out_specs=pl.BlockSpec((tm, D_MODEL), lambda i, ps: (i, 0)),
            scratch_shapes=[pltpu.SMEM((2 * tm,), I32), pltpu.SMEM((2 * tm,), I32),
                            pltpu.VMEM((2 * tm * ROW_TILES, LANES), F32), pltpu.SemaphoreType.DMA((3, 2))]),
        out_shape=jax.ShapeDtypeStruct((n, D_MODEL), F32),
        compiler_params=_params(("arbitrary",)),
    )(pstart, cls, rank, ys, h1, g)


def _moe_and_final(pool_o, attn_o, x, wo, g_ffn, wr, br, wg, wu, wd, g_final):
    n = x.shape[0]
    h1, xe, meta, counts = _outproj_router(pool_o, attn_o, x, wo, g_ffn, wr, br, jnp.zeros((1, LANES), F32))
    cnt = counts[0].astype(I32)
    padded = ((cnt + MOE_ROWS - 1) // MOE_ROWS) * MOE_ROWS
    pends = jnp.cumsum(padded)
    pstart = (pends - padded).astype(I32)
    n_blk = -(-n // MOE_ROWS) + N_CLASSES
    n_used = (pends[-1] // MOE_ROWS).astype(I32)
    blk = jnp.arange(n_blk, dtype=I32)
    blk_last = jnp.minimum(blk, jnp.maximum(n_used - 1, 0))
    blk_cls = jnp.sum((pends[None, :N_CLASSES] <= (blk_last * MOE_ROWS)[:, None]).astype(I32), axis=1)
    blk_cls = jnp.minimum(blk_cls, N_CLASSES - 1)
    blk_a = jnp.asarray(CLASS_A)[blk_cls]
    blk_b = jnp.asarray(CLASS_B)[blk_cls]
    cls, rank = meta[0], meta[1]
    xs = _dispatch(pstart, cls, rank, xe, jnp.zeros((n_blk * MOE_ROWS * XE_PITCH, LANES), F32))
    ys = _ffn(blk_a, blk_b, n_used.reshape(1), xs, wg, wu, wd)
    return _combine(pstart, cls, rank, ys, h1, g_final)


def kernel(x_prompt, x_sample, cache_k, cache_v, cache_idx_k, state_pool, page_table, norm_mix_g, w_in, pool_w,
           pool_scale, w_out, norm_ffn_g, router_group_w, router_group_b, router_expert_w, router_expert_b,
           expert_w_gate, expert_w_up, expert_w_down, final_norm_g):
    depth = w_in.shape[0]
    assert depth == 1, "single-layer step"
    b, s, d = x_prompt.shape
    db, t_new, _ = x_sample.shape
    n_pages = page_table.shape[1]
    past = n_pages * PAGE_SIZE
    n_pool = cache_k.shape[1]
    l = 0

    w_in_b = jnp.pad(w_in[l], ((0, 0), (0, IN_PAD - IN_WIDTH))).astype(BF16)
    g_mix = norm_mix_g[l].reshape(1, d)
    pw = pool_w[l].astype(BF16)
    psc = pool_scale[l].reshape(1, POOL_WIDTH)
    wo = w_out[l].astype(BF16)
    g_ffn = norm_ffn_g[l].reshape(1, d)
    wr = jnp.pad(jnp.concatenate([router_group_w[l], router_expert_w[l]], axis=1),
                 ((0, 0), (0, LANES - N_GROUPS - N_EXPERTS))).astype(BF16)
    br = jnp.pad(jnp.concatenate([router_group_b[l], router_expert_b[l]]), (0, LANES - N_GROUPS - N_EXPERTS)).reshape(1, LANES)
    wg = expert_w_gate[l].astype(BF16)
    wu = expert_w_up[l].astype(BF16)
    wd = expert_w_down[l].astype(BF16)
    g_fin = final_norm_g.reshape(1, d)
    moe = lambda pool_o, attn_o, x: _moe_and_final(pool_o, attn_o, x, wo, g_ffn, wr, br, wg, wu, wd, g_fin)

    xp = x_prompt.reshape(b * s, d)
    tab_p = _rope_tables(jnp.arange(s, dtype=I32))
    u, qt, k, v, qit, ki, kw, kb, vt, kwb = _inproj(xp, g_mix, w_in_b, tab_p, seq_len=s)
    r3 = lambda a: a.reshape(b, s, a.shape[-1])
    pool_o = _pool_prompt(r3(u), pw, psc)
    attn_o = _dsa_prompt(qt, qit, r3(kw), r3(kwb), r3(kb), vt, min(TOPK_MAX, s // 4))
    y_prompt = moe(pool_o.reshape(b * s, POOL_WIDTH), attn_o.reshape(b * s, ATTN_WIDTH), xp).reshape(b, s, d)
    k_prompt = jnp.moveaxis(k.reshape(b, N_KV_HEADS, HEAD_DIM, s), -1, 1)[None]
    v_prompt = jnp.moveaxis(v.reshape(b, N_KV_HEADS, HEAD_DIM, s), -1, 1)[None]
    idx_k_prompt = jnp.moveaxis(ki, -1, 1)[None]
    if s >= POOL_STATE:
        pool_prompt = r3(u)[:, s - POOL_STATE:][None]
    else:
        pool_prompt = jnp.concatenate([jnp.zeros((b, POOL_STATE, POOL_WIDTH), F32), r3(u)], axis=1)[:, -POOL_STATE:][None]

    xs_ = x_sample.reshape(db * t_new, d)
    tab_s = _rope_tables(past + (jnp.arange(db * t_new, dtype=I32) % t_new))
    u2, q2, k2, v2, qi2, ki2, kw2, _, _, _ = _inproj(xs_, g_mix, w_in_b, tab_s)
    u2_3 = u2.reshape(db, t_new, POOL_WIDTH)
    pool2 = _pool_sample(jnp.swapaxes(state_pool[l], 0, 1), jnp.swapaxes(u2_3, 0, 1), pw, psc, past)
    pool2 = jnp.swapaxes(pool2, 0, 1).reshape(db * t_new, POOL_WIDTH)
    page_t = lambda c: jnp.moveaxis(c[l], 1, -1).reshape(n_pool, -1, PAGE_SIZE)
    attn2 = _dsa_sample(page_table, q2, qi2, kw2, k2, v2, page_t(cache_idx_k), page_t(cache_k), page_t(cache_v),
                        min(TOPK_MAX, (past + t_new) // 4))
    y_sample = moe(pool2, attn2, xs_).reshape(db, t_new, d)
    k_sample = k2.reshape(1, db, t_new, N_KV_HEADS, HEAD_DIM)
    v_sample = v2.reshape(1, db, t_new, N_KV_HEADS, HEAD_DIM)
    idx_k_sample = ki2.reshape(1, db, t_new, IDX_DIM)
    pool_sample = jnp.concatenate([state_pool[l], u2_3], axis=1)[:, -POOL_STATE:][None]

    return (y_prompt, y_sample, k_prompt, v_prompt, idx_k_prompt, pool_prompt,
            k_sample, v_sample, idx_k_sample, pool_sample)
```

```python
import functools
import math

import numpy as np
import jax
import jax.numpy as jnp
from jax import lax
from jax.experimental import pallas as pl
from jax.experimental.pallas import tpu as pltpu

F32 = jnp.float32
BF16 = jnp.bfloat16
I32 = jnp.int32

D_MODEL = 1024
HEAD_DIM = 64
N_HEADS = 8
N_KV_HEADS = 2
ROPE_DIM = 16
ROPE_THETA = 500000.0
IDX_HEADS = 8
IDX_DIM = 64
TOPK_MAX = 256
PAGE_SIZE = 128
POOL_WIDTH = 512
POOL_WINDOWS = (2, 4, 8, 16)
POOL_GROUP = 128
POOL_STATE = 15
ATTN_WIDTH = 512
N_GROUPS = 4
EXPERTS_PER_GROUP = 8
N_EXPERTS = 32
EXPERT_HIDDEN = 256
NORM_EPS = 1e-6

LANES = 128
IN_WIDTH = 1864
IN_PAD = 1920
C_U, C_Q, C_K, C_V, C_QI, C_KW = 0, 512, 1024, 1152, 1280, 1792
PAIRS_PER_GROUP = EXPERTS_PER_GROUP * (EXPERTS_PER_GROUP - 1) // 2
N_CLASSES = N_GROUPS * PAIRS_PER_GROUP
MOE_ROWS = 128
ROW_TILES = D_MODEL // LANES
XE_PITCH = ROW_TILES + 1
ATT_SUB_HEADS = 4
VT_ROWS = LANES + 16
LOG2E = 1.4426950408889634
NEG = -1e30
INT_MIN = -2 ** 31
INT_MAX = 2 ** 31 - 1
VMEM_LIMIT = 48 * 1024 * 1024


def _class_tables():
    a, b = [], []
    for g in range(N_GROUPS):
        for lo in range(EXPERTS_PER_GROUP):
            for hi in range(lo + 1, EXPERTS_PER_GROUP):
                a.append(g * EXPERTS_PER_GROUP + lo)
                b.append(g * EXPERTS_PER_GROUP + hi)
    return np.array(a, np.int32), np.array(b, np.int32)


CLASS_A, CLASS_B = _class_tables()


def _tile(n, cap):
    t = min(n, cap)
    while n % t:
        t -= 8
    return t


def _params(sem):
    return pltpu.CompilerParams(dimension_semantics=sem, vmem_limit_bytes=VMEM_LIMIT)


def _rms(x, g):
    return x * lax.rsqrt(jnp.mean(x * x, axis=-1, keepdims=True) + NORM_EPS) * g


def _dot_nt(a, b):
    return lax.dot_general(a, b, (((1,), (1,)), ((), ())), preferred_element_type=F32)


def _tree(parts, op):
    parts = list(parts)
    while len(parts) > 1:
        nxt = [op(parts[a], parts[a + 1]) for a in range(0, len(parts) - 1, 2)]
        if len(parts) % 2:
            nxt.append(parts[-1])
        parts = nxt
    return parts[0]


def _fold_rows(x, op):
    return _tree([x[r:r + 8] for r in range(0, x.shape[0], 8)], op)


def _count_rows(mask, n_acc=8):
    accs = [jnp.zeros((8, mask.shape[1]), F32) for _ in range(n_acc)]
    for a, r in enumerate(range(0, mask.shape[0], 8)):
        acc = accs[a % n_acc]
        accs[a % n_acc] = jnp.where(mask[r:r + 8], acc + 1.0, acc)
    return _fold_rows(jnp.concatenate(accs, axis=0), jnp.add)


def _sort_key(score):
    bits = lax.bitcast_convert_type(score, I32)
    return jnp.where(bits < 0, bits ^ jnp.int32(INT_MAX), bits)


def _rope_tables(pos):
    half = ROPE_DIM // 2
    r = pos.shape[0]
    inv = jnp.power(jnp.float32(ROPE_THETA), -jnp.arange(half, dtype=F32) / half)
    ang = pos.astype(F32)[:, None] * inv[None, :]
    cos, sin = jnp.cos(ang), jnp.sin(ang)
    one = lambda n: jnp.ones((r, n), F32)
    zero = lambda n: jnp.zeros((r, n), F32)
    c64 = jnp.concatenate([cos, cos, one(HEAD_DIM - ROPE_DIM)], axis=1)
    s1_64 = jnp.concatenate([-sin, zero(HEAD_DIM - half)], axis=1)
    s2_64 = jnp.concatenate([zero(half), sin, zero(HEAD_DIM - ROPE_DIM)], axis=1)
    ck = jnp.concatenate([c64, jnp.full((r, IDX_HEADS), IDX_HEADS ** -0.5, F32), one(LANES - HEAD_DIM - IDX_HEADS)], axis=1)
    return jnp.concatenate([c64, c64, s1_64, s1_64, s2_64, s2_64, ck, s1_64, zero(64), s2_64, zero(64)], axis=1)


def _inproj_kernel(x_ref, g_ref, w_ref, tab_ref, u_ref, q_ref, k_ref, v_ref, qi_ref, ki_ref, kw_ref,
                   kb_ref, vb_ref, kwb_ref, *, transposed):
    xb = _rms(x_ref[...], g_ref[...]).astype(BF16)
    tm = xb.shape[0]
    lane = lax.broadcasted_iota(I32, (tm, LANES), 1)

    def seg(c0, c1):
        return jnp.dot(xb, w_ref[:, c0:c1], preferred_element_type=F32)

    def rope(z, t0):
        c, s1, s2 = (tab_ref[:, t0 + i * LANES:t0 + (i + 1) * LANES] for i in range(3))
        return z * c + pltpu.roll(z, LANES - ROPE_DIM // 2, 1) * s1 + pltpu.roll(z, ROPE_DIM // 2, 1) * s2

    zs = {c0: seg(c0, c1) for c0, c1 in ((C_U, C_Q), (C_Q, C_K), (C_K, C_V), (C_V, C_QI), (C_QI, C_KW), (C_KW, IN_PAD))}
    u_ref[...] = zs[C_U]

    for c_base, out_ref, to_group in ((C_Q, q_ref, True), (C_QI, qi_ref, False)):
        z = zs[c_base]
        scale = HEAD_DIM ** -0.5 * (LOG2E if (transposed and to_group) else 1.0)
        for p in range(N_HEADS // 2):
            zz = rope(z[:, p * LANES:(p + 1) * LANES], 0) * scale
            zr = pltpu.roll(zz, HEAD_DIM, 1)
            for hh in range(2):
                h = 2 * p + hh
                tgt = (h // (N_HEADS // N_KV_HEADS)) if to_group else 0
                src = zz if hh == tgt else zr
                keep = (lane >= tgt * HEAD_DIM) & (lane < (tgt + 1) * HEAD_DIM)
                val = jnp.where(keep, src, 0.0)
                if transposed:
                    for qb in range(tm // LANES):
                        blk = jnp.transpose(val[qb * LANES:(qb + 1) * LANES])
                        out_ref[0, qb, :, h * LANES:(h + 1) * LANES] = blk.astype(out_ref.dtype)
                else:
                    out_ref[:, h * LANES:(h + 1) * LANES] = val.astype(out_ref.dtype)

    k = rope(zs[C_K], 0)
    kb_ref[...] = k.astype(BF16)
    v = zs[C_V]
    kw = rope(zs[C_KW], 3 * LANES)
    kw_ref[...] = kw
    kwb_ref[...] = kw.astype(BF16)
    if transposed:
        for qb in range(tm // LANES):
            cols = slice(qb * LANES, (qb + 1) * LANES)
            k_ref[0, :, cols] = jnp.transpose(k[cols])
            vt = jnp.transpose(v[cols])
            v_ref[0, :, cols] = vt
            vb_ref[0, 0, 0:LANES, cols] = vt.astype(BF16)
            ki_ref[0, :, cols] = jnp.transpose(kw[cols])[:IDX_DIM]
        vb_ref[0, 0, LANES:, :] = jnp.ones((VT_ROWS - LANES, tm), BF16)
    else:
        k_ref[...] = k
        v_ref[...] = v
        vb_ref[...] = v.astype(BF16)
        ki_ref[...] = kw[:, :IDX_DIM]


def _inproj(x, g, w, tab, seq_len=None):
    n = x.shape[0]
    transposed = seq_len is not None
    tm = _tile(seq_len, 512) if transposed else _tile(n, 512)
    n_tab = tab.shape[0] // tm
    row = lambda width: pl.BlockSpec((tm, width), lambda i: (i, 0))
    rows = lambda width, dt: (row(width), jax.ShapeDtypeStruct((n, width), dt))
    if transposed:
        assert tm % LANES == 0
        nb, ns, qpt = n // seq_len, seq_len // tm, tm // LANES
        qt = (pl.BlockSpec((1, qpt, LANES, N_HEADS * LANES), lambda i: (i // ns, i % ns, 0, 0)),
              jax.ShapeDtypeStruct((nb, seq_len // LANES, LANES, N_HEADS * LANES), BF16))
        vt = (pl.BlockSpec((1, 1, VT_ROWS, tm), lambda i: (i // ns, i % ns, 0, 0)),
              jax.ShapeDtypeStruct((nb, ns, VT_ROWS, tm), BF16))
        tr = lambda width: (pl.BlockSpec((1, width, tm), lambda i: (i // ns, 0, i % ns)),
                            jax.ShapeDtypeStruct((nb, width, seq_len), F32))
        kf, vf, kif = tr(LANES), tr(LANES), tr(IDX_DIM)
    else:
        qt = rows(N_HEADS * LANES, F32)
        vt = rows(LANES, BF16)
        kf, vf, kif = rows(LANES, F32), rows(LANES, F32), rows(IDX_DIM, F32)
    outs = [rows(POOL_WIDTH, F32), qt, kf, vf, qt, kif, rows(LANES, F32), rows(LANES, BF16), vt, rows(LANES, BF16)]
    return pl.pallas_call(
        functools.partial(_inproj_kernel, transposed=transposed),
        grid=(n // tm,),
        in_specs=[row(D_MODEL), pl.BlockSpec((1, D_MODEL), lambda i: (0, 0)),
                  pl.BlockSpec((D_MODEL, IN_PAD), lambda i: (0, 0)),
                  pl.BlockSpec((tm, 6 * LANES), lambda i: (i % n_tab, 0))],
        out_specs=[spec for spec, _ in outs],
        out_shape=[shape for _, shape in outs],
        compiler_params=_params(("arbitrary",)),
    )(x, g, w, tab)


def _pool_prompt_kernel(u_ref, pw_ref, sc_ref, o_ref, ext_ref):
    t = pl.program_id(1)
    tp = u_ref.shape[1]
    halo = POOL_STATE + 1

    @pl.when(t == 0)
    def _():
        ext_ref[0:halo, :] = jnp.zeros((halo, POOL_WIDTH), F32)

    @pl.when(t > 0)
    def _():
        ext_ref[0:halo, :] = ext_ref[tp:tp + halo, :]

    ext_ref[halo:halo + tp, :] = u_ref[0]
    pos = t * tp + lax.broadcasted_iota(I32, (tp, 1), 0)
    for g, w in enumerate(POOL_WINDOWS):
        c0 = g * POOL_GROUP
        cur = ext_ref[halo:halo + tp, c0:c0 + POOL_GROUP]
        win = cur
        for j in range(1, w):
            win = win + ext_ref[halo - j:halo - j + tp, c0:c0 + POOL_GROUP]
        cnt = jnp.minimum(pos + 1, w).astype(F32)
        d = (win / cnt - cur).astype(BF16)
        o = jnp.dot(d, pw_ref[g], preferred_element_type=F32) * sc_ref[:, c0:c0 + POOL_GROUP]
        o_ref[0, :, c0:c0 + POOL_GROUP] = o.astype(o_ref.dtype)


def _pool_prompt(u, pw, sc):
    b, s, _ = u.shape
    tp = _tile(s, 512)
    return pl.pallas_call(
        _pool_prompt_kernel,
        grid=(b, s // tp),
        in_specs=[pl.BlockSpec((1, tp, POOL_WIDTH), lambda i, t: (i, t, 0)),
                  pl.BlockSpec((len(POOL_WINDOWS), POOL_GROUP, POOL_GROUP), lambda i, t: (0, 0, 0)),
                  pl.BlockSpec((1, POOL_WIDTH), lambda i, t: (0, 0))],
        out_specs=pl.BlockSpec((1, tp, POOL_WIDTH), lambda i, t: (i, t, 0)),
        out_shape=jax.ShapeDtypeStruct((b, s, POOL_WIDTH), BF16),
        scratch_shapes=[pltpu.VMEM((tp + POOL_STATE + 1, POOL_WIDTH), F32)],
        compiler_params=_params(("arbitrary", "arbitrary")),
    )(u, pw, sc)


def _pool_sample_kernel(st_ref, u_ref, pw_ref, sc_ref, o_ref, *, pos0):
    t_new = u_ref.shape[0]
    ext = [st_ref[i] for i in range(POOL_STATE)] + [u_ref[i] for i in range(t_new)]
    for g, w in enumerate(POOL_WINDOWS):
        c0 = g * POOL_GROUP
        ds = []
        for t in range(t_new):
            e = POOL_STATE + t
            cur = ext[e][:, c0:c0 + POOL_GROUP]
            win = cur
            for j in range(1, w):
                win = win + ext[e - j][:, c0:c0 + POOL_GROUP]
            cnt = float(min(pos0 + t + 1, w))
            ds.append((win / cnt - cur).astype(BF16))
        d = jnp.concatenate(ds, axis=0)
        o = jnp.dot(d, pw_ref[g], preferred_element_type=F32) * sc_ref[:, c0:c0 + POOL_GROUP]
        db = o.shape[0] // t_new
        for t in range(t_new):
            o_ref[t, :, c0:c0 + POOL_GROUP] = o[t * db:(t + 1) * db]


def _pool_sample(st_t, u_t, pw, sc, pos0):
    t_new, db, _ = u_t.shape
    return pl.pallas_call(
        functools.partial(_pool_sample_kernel, pos0=pos0),
        out_shape=jax.ShapeDtypeStruct((t_new, db, POOL_WIDTH), F32),
        compiler_params=pltpu.CompilerParams(vmem_limit_bytes=VMEM_LIMIT),
    )(st_t, u_t, pw, sc)


def _search_threshold(count_ge, shape, topk):
    def bit_body(it, t):
        cand = t + lax.shift_left(jnp.int32(1), 31 - it)
        return jnp.where(count_ge(cand) >= topk, cand, t)
    return lax.fori_loop(0, 32, bit_body, jnp.full(shape, INT_MIN, I32))


def _search_threshold_2bit(count_ge, shape, topk):
    def pair_body(it, t):
        step = lax.shift_left(jnp.int32(1), 30 - 2 * it)
        hits = jnp.zeros(shape, I32)
        for j in (1, 2, 3):
            hits = hits + jnp.where(count_ge(t + step * j) >= topk, 1, 0)
        return t + step * hits
    return lax.fori_loop(0, 16, pair_body, jnp.full(shape, INT_MIN, I32))


def _search_threshold16(count_ge, need, width):
    def bit_body(it, t):
        cand = t + lax.shift_left(jnp.int32(1), 15 - it)
        return jnp.where(count_ge(cand) >= need, cand, t)
    return lax.fori_loop(0, 16, bit_body, jnp.full((1, width), -32768, I32))


def _pack_halves(x):
    half = x.shape[0] // 2
    return (x[:half] & 0xFFFF) | (x[half:] << 16)


def _search_tie_index(count_eq_below, need, shape, n_bits):
    def bit_body(it, j):
        cand = j + lax.shift_left(jnp.int32(1), n_bits - 1 - it)
        return jnp.where(count_eq_below(cand) < need, cand, j)
    return lax.fori_loop(0, n_bits, bit_body, jnp.zeros(shape, I32))


def _place_heads(o_ref_store, heads, lane):
    for p in range(N_HEADS // 2):
        parts = []
        for hh in range(2):
            h = 2 * p + hh
            g = h // (N_HEADS // N_KV_HEADS)
            parts.append(heads[h] if g == hh else pltpu.roll(heads[h], HEAD_DIM, 1))
        o_ref_store(p, jnp.where(lane < HEAD_DIM, parts[0], parts[1]))


def _dsa_prompt_kernel(qt_ref, qit_ref, kwq_ref, kwb_ref, kb_ref, vt_ref, o_ref, key_s, hi_s, lo_s, j_s,
                       *, topk, ck, n_bits):
    i = pl.program_id(1)
    tq = LANES
    nc = (i * tq) // ck + 1
    qpos = i * tq + lax.broadcasted_iota(I32, (ck, tq), 1)
    row = lax.broadcasted_iota(I32, (ck, tq), 0)
    hpg = N_HEADS // N_KV_HEADS

    wit =jnp.transpose(kwq_ref[0])[IDX_DIM:IDX_DIM + IDX_HEADS, :]

    def chunk_loop(body, init):
        pairs = nc // 2
        carry = lax.fori_loop(0, pairs, lambda j, cr: body((2 * j, 2 * j + 1), cr), init)
        return lax.fori_loop(2 * pairs, nc, lambda c, cr: body((c,), cr), carry)

    def score_body(chunks, carry):
        sts = [jnp.dot(kwb_ref[0, pl.ds(pl.multiple_of(c * ck, ck), ck), :], qit_ref[0, 0],
                       preferred_element_type=F32) for c in chunks]
        for c, st in zip(chunks, sts):
            acc = jnp.zeros((ck, tq), F32)
            for h in range(IDX_HEADS):
                acc = acc + wit[h:h + 1, :] * jnp.maximum(st[:, h * tq:(h + 1) * tq], 0.0)
            score = jnp.where(c * ck + row <= qpos, acc + 0.0, -jnp.inf)
            key = _sort_key(score)
            key_s[c] = key
            hi_s[c] = _pack_halves(key >> 16)
        return carry

    chunk_loop(score_body, 0)

    def count(pred):
        def body(c, acc):
            return acc + _count_rows(pred(key_s[c], c * ck + row))
        acc = lax.fori_loop(0, nc, body, jnp.zeros((8, tq), F32))
        return jnp.sum(acc, axis=0, keepdims=True)

    def count16(packed_s, cand, strict=False):
        c16 = pltpu.bitcast(jnp.broadcast_to(_pack_halves(jnp.concatenate([cand, cand], axis=0)), (8, tq)), jnp.int16)

        def body(c, accs):
            x16 = pltpu.bitcast(packed_s[c], jnp.int16)
            accs = list(accs)
            for a, r in enumerate(range(0, ck, 16)):
                x = x16[r:r + 16]
                hit = (x > c16) if strict else (x >= c16)
                accs[a % 8] = jnp.where(hit, accs[a % 8] + jnp.int16(1), accs[a % 8])
            return tuple(accs)

        accs = lax.fori_loop(0, nc, body, tuple(jnp.zeros((16, tq), jnp.int16) for _ in range(8)))
        w = pltpu.bitcast(_tree(accs, jnp.add), I32)
        return jnp.sum((((w << 16) >> 16) + (w >> 16)).astype(F32), axis=0, keepdims=True)

    t_hi = _search_threshold16(lambda cand: count16(hi_s, cand), float(topk), tq)
    c_above = count16(hi_s, t_hi, strict=True)

    def band_body(c, carry):
        key = key_s[c]
        lo = (key & 0xFFFF) - 32768
        lo_s[c] = _pack_halves(jnp.where((key >> 16) == t_hi, lo, -32768))
        return carry

    lax.fori_loop(0, nc, band_body, 0)
    t_lo = _search_threshold16(lambda cand: count16(lo_s, cand), float(topk) - c_above, tq)
    thr = t_hi * 65536 + (t_lo + 32768)
    c_gt = c_above + count16(lo_s, t_lo, strict=True)
    c_ge = c_above + count16(lo_s, t_lo)
    need = float(topk) - c_gt
    j_s[...] = jnp.full((1, tq), INT_MAX, I32)

    @pl.when(jnp.max(c_ge - c_gt - need) > 0.0)
    def _():
        j_s[...] = _search_tie_index(lambda cand: count(lambda k, idx: (k == thr) & (idx < cand)),
                                     need, (1, tq), n_bits)

    jmax = j_s[...]

    hps = ATT_SUB_HEADS
    sw = hps * tq
    n_sub = N_HEADS // hps

    def att_body(chunks, carry):
        sts = []
        for c in chunks:
            k = key_s[c]
            idx = c * ck + row
            sel = (k > thr) | ((k == thr) & (idx <= jmax))
            b = jnp.where(sel & (idx <= qpos), 0.0, NEG)
            bias = jnp.concatenate([b] * hps, axis=1)
            kc = kb_ref[0, pl.ds(pl.multiple_of(c * ck, ck), ck), :]
            sts.append([jnp.dot(kc, qt_ref[0, 0, :, sb * sw:(sb + 1) * sw], preferred_element_type=F32) + bias
                        for sb in range(n_sub)])
        ms = [cr[0] for cr in carry]
        ps, alphas = [], []
        for st_c in sts:
            ps.append([])
            alphas.append([])
            for sb in range(n_sub):
                m_new = jnp.maximum(ms[sb], jnp.max(_fold_rows(st_c[sb], jnp.maximum), axis=0, keepdims=True))
                ps[-1].append(jnp.exp2(st_c[sb] - m_new).astype(BF16))
                alphas[-1].append(jnp.exp2(ms[sb] - m_new))
                ms[sb] = m_new
        accs = [cr[1] for cr in carry]
        for c, p_c, a_c in zip(chunks, ps, alphas):
            vc = vt_ref[0, c]
            for sb in range(n_sub):
                accs[sb] = a_c[sb] * accs[sb] + jnp.dot(vc, p_c[sb], preferred_element_type=F32)
        return tuple((ms[sb], accs[sb]) for sb in range(n_sub))

    init = (jnp.full((1, sw), NEG, F32), jnp.zeros((VT_ROWS, sw), F32))
    res = chunk_loop(att_body, (init,) * n_sub)
    for sb in range(n_sub):
        acc = res[sb][1]
        g = (sb * hps) // hpg
        ot = acc[g * HEAD_DIM:(g + 1) * HEAD_DIM, :] / acc[LANES:LANES + 1, :]
        for r in range(0, hps, 2):
            pair = jnp.concatenate([ot[:, r * tq:(r + 1) * tq], ot[:, (r + 1) * tq:(r + 2) * tq]], axis=0)
            c0 = (sb * hps + r) * HEAD_DIM
            o_ref[0, :, c0:c0 + LANES] = jnp.transpose(pair).astype(o_ref.dtype)


def _dsa_prompt(qt, qit, kw, kwb, kb, vt, topk):
    b, s, _ = kb.shape
    tq = LANES
    ck = vt.shape[-1]
    assert ck >= topk and ck % tq == 0 and s % ck == 0
    n_bits = max(1, int(math.ceil(math.log2(s))) + 1)
    qspec = pl.BlockSpec((1, 1, LANES, N_HEADS * LANES), lambda bi, i: (bi, i, 0, 0))
    kspec = pl.BlockSpec((1, s, LANES), lambda bi, i: (bi, 0, 0))
    return pl.pallas_call(
        functools.partial(_dsa_prompt_kernel, topk=topk, ck=ck, n_bits=n_bits),
        grid=(b, s // tq),
        in_specs=[qspec, qspec, pl.BlockSpec((1, tq, LANES), lambda bi, i: (bi, i, 0)), kspec, kspec,
                  pl.BlockSpec((1, s // ck, VT_ROWS, ck), lambda bi, i: (bi, 0, 0, 0))],
        out_specs=pl.BlockSpec((1, tq, ATTN_WIDTH), lambda bi, i: (bi, i, 0)),
        out_shape=jax.ShapeDtypeStruct((b, s, ATTN_WIDTH), BF16),
        scratch_shapes=[pltpu.VMEM((s // ck, ck, tq), I32), pltpu.VMEM((s // ck, ck // 2, tq), I32),
                        pltpu.VMEM((s // ck, ck // 2, tq), I32), pltpu.VMEM((1, tq), I32)],
        compiler_params=_params(("arbitrary", "arbitrary")),
    )(qt, qit, kw, kwb, kb, vt)


def _dsa_sample_kernel(pt_ref, q_ref, qi_ref, kw_ref, k_ref, v_ref, cik_hbm, ck_hbm, cv_hbm, o_ref,
                       ki_buf, k_buf, v_buf, sems, j_s, *, n_pages, topk, n_bits):
    b = pl.program_id(0)
    nb = pl.num_programs(0)
    slot = b % 2
    t_new = q_ref.shape[0]
    past = n_pages * PAGE_SIZE
    srcs = (cik_hbm, ck_hbm, cv_hbm)
    bufs = (ki_buf, k_buf, v_buf)

    def page_copy(a, page, sl, p):
        return pltpu.make_async_copy(srcs[a].at[page], bufs[a].at[sl, p], sems.at[a, sl])

    def start_all(bb, sl):
        def body(p, carry):
            page = pt_ref[bb * n_pages + p]
            for a in range(3):
                page_copy(a, page, sl, p).start()
            return carry
        lax.fori_loop(0, n_pages, body, 0)

    def wait_all(sl):
        def body(p, carry):
            for a in range(3):
                page_copy(a, 0, sl, p).wait()
            return carry
        lax.fori_loop(0, n_pages, body, 0)

    @pl.when(b == 0)
    def _():
        start_all(0, 0)

    @pl.when(b + 1 < nb)
    def _():
        start_all(b + 1, 1 - slot)

    wait_all(slot)

    pad_rows = LANES - t_new
    tq = lax.broadcasted_iota(I32, (t_new, 1), 0)
    lane_n = lax.broadcasted_iota(I32, (t_new, LANES), 1)
    idx_p = (lax.broadcasted_iota(I32, (n_pages, t_new, PAGE_SIZE), 0) * PAGE_SIZE
             + lax.broadcasted_iota(I32, (n_pages, t_new, PAGE_SIZE), 2))
    rep = lambda x: jnp.concatenate([x] * N_HEADS, axis=x.ndim - 2)
    paged = lambda a: jnp.broadcast_to(a[None], (n_pages,) + a.shape)
    bmm = lambda a, b_, nt: lax.dot_general(a, b_, (((2,), (2 if nt else 1,)), ((0,), (0,))),
                                            preferred_element_type=F32)
    over_pages = lambda x, op: _tree([x[p] for p in range(n_pages)], op)

    qi = jnp.concatenate([qi_ref[:, h * LANES:h * LANES + IDX_DIM] for h in range(IDX_HEADS)], axis=0).astype(BF16)
    wi = kw_ref[:, IDX_DIM:IDX_DIM + IDX_HEADS]
    wcol = jnp.concatenate([wi[:, h:h + 1] for h in range(IDX_HEADS)], axis=0)
    ki_new = jnp.concatenate([kw_ref[:, :IDX_DIM], jnp.zeros((pad_rows, IDX_DIM), F32)], axis=0).astype(BF16)

    def head_sum(r):
        acc = r[..., 0:t_new, :]
        for h in range(1, IDX_HEADS):
            acc = acc + r[..., h * t_new:(h + 1) * t_new, :]
        return acc + 0.0

    score_p = head_sum(wcol * jnp.maximum(bmm(paged(qi), ki_buf[slot].astype(BF16), False), 0.0))
    score_n = jnp.where(lane_n <= tq, head_sum(wcol * jnp.maximum(_dot_nt(qi, ki_new), 0.0)), -jnp.inf)
    key_p = _sort_key(score_p)
    key_n = _sort_key(score_n)
    idx_n = past + lane_n

    def count(pred):
        cp = over_pages(jnp.where(pred(key_p, idx_p), 1.0, 0.0), jnp.add)
        return jnp.sum(cp + jnp.where(pred(key_n, idx_n), 1.0, 0.0), axis=1, keepdims=True)

    thr = _search_threshold_2bit(lambda cand: count(lambda k, idx: k >= cand), (t_new, 1), float(topk))
    c_gt = count(lambda k, idx: k > thr)
    c_ge = count(lambda k, idx: k >= thr)
    need = float(topk) - c_gt
    j_s[...] = jnp.full((t_new, 1), INT_MAX, I32)

    @pl.when(jnp.max(c_ge - c_gt - need) > 0.0)
    def _():
        j_s[...] = _search_tie_index(lambda cand: count(lambda k, idx: (k == thr) & (idx < cand)),
                                     need, (t_new, 1), n_bits)

    jmax = j_s[...]
    sel = lambda k, idx: (k > thr) | ((k == thr) & (idx <= jmax))
    bias_p = jnp.where(sel(key_p, idx_p), 0.0, NEG)
    bias_n = jnp.where(sel(key_n, idx_n) & (lane_n <= tq), 0.0, NEG)

    q = jnp.concatenate([q_ref[:, h * LANES:(h + 1) * LANES] for h in range(N_HEADS)], axis=0).astype(BF16)
    zpad = jnp.zeros((pad_rows, LANES), F32)
    k_new = jnp.concatenate([k_ref[...], zpad], axis=0).astype(BF16)
    v_new = jnp.concatenate([v_ref[...], zpad], axis=0).astype(BF16)
    s_p = bmm(paged(q), k_buf[slot].astype(BF16), False) + rep(bias_p)
    s_n = _dot_nt(q, k_new) + rep(bias_n)
    m = jnp.max(jnp.maximum(over_pages(s_p, jnp.maximum), s_n), axis=1, keepdims=True)
    p_p = jnp.exp(s_p - m)
    p_n = jnp.exp(s_n - m)
    l = jnp.sum(over_pages(p_p, jnp.add) + p_n, axis=1, keepdims=True)
    o = (over_pages(bmm(p_p.astype(BF16), v_buf[slot].astype(BF16), True), jnp.add)
         + jnp.dot(p_n.astype(BF16), v_new, preferred_element_type=F32)) / l
    heads = [o[h * t_new:(h + 1) * t_new] for h in range(N_HEADS)]

    def store(p, val):
        o_ref[:, p * LANES:(p + 1) * LANES] = val

    _place_heads(store, heads, lane_n)


def _dsa_sample(page_table, q, qi, kw, k, v, cik, ck, cv, topk):
    db, n_pages = page_table.shape
    t_new = q.shape[0] // db
    past = n_pages * PAGE_SIZE
    n_bits = max(1, int(math.ceil(math.log2(past + LANES))) + 1)
    row = lambda width: pl.BlockSpec((t_new, width), lambda b, pt: (b, 0))
    any_spec = pl.BlockSpec(memory_space=pl.ANY)
    return pl.pallas_call(
        functools.partial(_dsa_sample_kernel, n_pages=n_pages, topk=topk, n_bits=n_bits),
        grid_spec=pltpu.PrefetchScalarGridSpec(
            num_scalar_prefetch=1,
            grid=(db,),
            in_specs=[row(N_HEADS * LANES), row(IDX_HEADS * LANES), row(LANES), row(LANES), row(LANES),
                      any_spec, any_spec, any_spec],
            out_specs=row(ATTN_WIDTH),
            scratch_shapes=[pltpu.VMEM((2, n_pages, IDX_DIM, PAGE_SIZE), F32),
                            pltpu.VMEM((2, n_pages, N_KV_HEADS * HEAD_DIM, PAGE_SIZE), F32),
                            pltpu.VMEM((2, n_pages, N_KV_HEADS * HEAD_DIM, PAGE_SIZE), F32),
                            pltpu.SemaphoreType.DMA((3, 2)),
                            pltpu.VMEM((t_new, 1), I32)]),
        out_shape=jax.ShapeDtypeStruct((db * t_new, ATTN_WIDTH), F32),
        compiler_params=pltpu.CompilerParams(dimension_semantics=("arbitrary",), vmem_limit_bytes=56 * 1024 * 1024),
    )(page_table.reshape(-1), q, qi, kw, k, v, cik, ck, cv)


def _outproj_router_kernel(pool_ref, attn_ref, x_ref, wo_ref, g_ref, wr_ref, br_ref, cin_ref,
                           h_ref, xe_ref, meta_ref, cnt_ref, carry_s):
    i = pl.program_id(0)
    tm = x_ref.shape[0]

    @pl.when(i == 0)
    def _():
        carry_s[...] = cin_ref[...]

    h1 = (x_ref[...]
          + jnp.dot(pool_ref[...].astype(BF16), wo_ref[0:POOL_WIDTH, :], preferred_element_type=F32)
          + jnp.dot(attn_ref[...].astype(BF16), wo_ref[POOL_WIDTH:, :], preferred_element_type=F32))
    h_ref[...] = h1
    xn = _rms(h1, g_ref[...])
    for j in range(ROW_TILES):
        xe_ref[pl.ds(j, tm, stride=XE_PITCH), :] = xn[:, j * LANES:(j + 1) * LANES]
    logits = jnp.dot(xn.astype(BF16), wr_ref[...], preferred_element_type=F32) + br_ref[...]

    lane = lax.broadcasted_iota(I32, (tm, LANES), 1).astype(F32)
    rmax = lambda v: jnp.max(v, axis=1, keepdims=True)
    rmin = lambda v: jnp.min(v, axis=1, keepdims=True)
    rsum = lambda v: jnp.sum(v, axis=1, keepdims=True)

    is_g = lane < N_GROUPS
    gl = jnp.where(is_g, logits, -jnp.inf)
    gmax = rmax(gl)
    g_sel = rmin(jnp.where(gl == gmax, lane, float(LANES)))
    g_w = 1.0 / rsum(jnp.where(is_g, jnp.exp(gl - gmax), 0.0))
    e0 = N_GROUPS + g_sel * EXPERTS_PER_GROUP
    is_e = (lane >= e0) & (lane < e0 + EXPERTS_PER_GROUP)
    el = jnp.where(is_e, logits, -jnp.inf)
    un = jnp.where(is_e, jnp.exp(el - rmax(el)), 0.0)
    prob = jnp.where(is_e, un / rsum(un), -1.0)
    p0 = rmax(prob)
    i0 = rmin(jnp.where(prob == p0, lane, float(LANES)))
    prob1 = jnp.where(lane == i0, -1.0, prob)
    p1 = rmax(prob1)
    i1 = rmin(jnp.where(prob1 == p1, lane, float(LANES)))
    w0 = g_w * p0 / (p0 + p1)
    w1 = g_w * p1 / (p0 + p1)
    lo = jnp.minimum(i0, i1) - e0
    hi = jnp.maximum(i0, i1) - e0
    cls = g_sel * PAIRS_PER_GROUP + lo * (2 * EXPERTS_PER_GROUP - 1 - lo) * 0.5 + hi - lo - 1.0
    w_a = jnp.where(i0 < i1, w0, w1)
    w_b = jnp.where(i0 < i1, w1, w0)
    xe_ref[pl.ds(ROW_TILES, tm, stride=XE_PITCH), :] = jnp.where(lane == 0.0, w_a, jnp.where(lane == 1.0, w_b, 0.0))

    onehot = lane == cls
    tri = (lax.broadcasted_iota(I32, (tm, tm), 1) < lax.broadcasted_iota(I32, (tm, tm), 0)).astype(BF16)
    before = jnp.dot(tri, onehot.astype(BF16), preferred_element_type=F32) + carry_s[...]
    rank = rsum(jnp.where(onehot, before, 0.0))
    carry_s[...] = carry_s[...] + jnp.sum(onehot.astype(F32), axis=0, keepdims=True)
    cnt_ref[...] = carry_s[...]
    meta = jnp.where(lane == 0.0, cls, jnp.where(lane == 1.0, rank, 0.0))
    meta_ref[...] = jnp.transpose(meta)[0:8, :].astype(I32)


def _outproj_router(pool_o, attn_o, x, wo, g, wr, br, counts_in):
    n = x.shape[0]
    tm = _tile(n, 512)
    row = lambda width: pl.BlockSpec((tm, width), lambda i: (i, 0))
    full = lambda shape: pl.BlockSpec(shape, lambda i: (0,) * len(shape))
    return pl.pallas_call(
        _outproj_router_kernel,
        grid=(n // tm,),
        in_specs=[row(POOL_WIDTH), row(ATTN_WIDTH), row(D_MODEL), full((D_MODEL, D_MODEL)), full((1, D_MODEL)),
                  full((D_MODEL, LANES)), full((1, LANES)), full((1, LANES))],
        out_specs=[row(D_MODEL), pl.BlockSpec((tm * XE_PITCH, LANES), lambda i: (i, 0)),
                   pl.BlockSpec((8, tm), lambda i: (0, i)), full((1, LANES))],
        out_shape=[jax.ShapeDtypeStruct((n, D_MODEL), F32), jax.ShapeDtypeStruct((n * XE_PITCH, LANES), F32),
                   jax.ShapeDtypeStruct((8, n), I32), jax.ShapeDtypeStruct((1, LANES), F32)],
        scratch_shapes=[pltpu.VMEM((1, LANES), F32)],
        compiler_params=_params(("arbitrary",)),
    )(pool_o, attn_o, x, wo, g, wr, br, counts_in)


def _load_dest(ps_ref, cls_hbm, rank_hbm, cls_s, rank_s, sem, base, tm):
    c1 = pltpu.make_async_copy(cls_hbm.at[pl.ds(base, tm)], cls_s, sem.at[0])
    c2 = pltpu.make_async_copy(rank_hbm.at[pl.ds(base, tm)], rank_s, sem.at[1])
    c1.start()
    c2.start()
    c1.wait()
    c2.wait()
    return lambda r: ps_ref[cls_s[r]] + rank_s[r]


def _row_dmas(tm, make_copy, dest):
    def start(r, carry):
        make_copy(r, dest(r)).start()
        return carry

    def wait(r, carry):
        make_copy(r, 0).wait()
        return carry

    lax.fori_loop(0, tm, start, 0, unroll=8)
    lax.fori_loop(0, tm, wait, 0, unroll=8)


def _dispatch_kernel(ps_ref, cls_hbm, rank_hbm, xe_ref, xs_in_hbm, xs_hbm, cls_s, rank_s, sem, *, tm):
    del xs_in_hbm
    base = pl.program_id(0) * tm
    dest = _load_dest(ps_ref, cls_hbm, rank_hbm, cls_s, rank_s, sem, base, tm)
    _row_dmas(tm, lambda r, d: pltpu.make_async_copy(xe_ref.at[pl.ds(r * XE_PITCH, XE_PITCH)],
                                                     xs_hbm.at[pl.ds(d * XE_PITCH, XE_PITCH)], sem.at[2]), dest)


def _dispatch(pstart, cls, rank, xe, xs):
    n = xe.shape[0] // XE_PITCH
    tm = _tile(n, 512)
    any_spec = pl.BlockSpec(memory_space=pl.ANY)
    return pl.pallas_call(
        functools.partial(_dispatch_kernel, tm=tm),
        grid_spec=pltpu.PrefetchScalarGridSpec(
            num_scalar_prefetch=1, grid=(n // tm,),
            in_specs=[any_spec, any_spec, pl.BlockSpec((tm * XE_PITCH, LANES), lambda i, ps: (i, 0)), any_spec],
            out_specs=any_spec,
            scratch_shapes=[pltpu.SMEM((tm,), I32), pltpu.SMEM((tm,), I32), pltpu.SemaphoreType.DMA((3,))]),
        out_shape=jax.ShapeDtypeStruct(xs.shape, xs.dtype),
        input_output_aliases={4: 0},
        compiler_params=pltpu.CompilerParams(dimension_semantics=("arbitrary",), has_side_effects=True,
                                             vmem_limit_bytes=VMEM_LIMIT),
    )(pstart, cls, rank, xe, xs)


def _ffn_kernel(ba_ref, bb_ref, nu_ref, xs_ref, wga_ref, wua_ref, wda_ref, wgb_ref, wub_ref, wdb_ref, ys_ref):
    j = pl.program_id(0)
    rows = MOE_ROWS

    @pl.when(j < nu_ref[0])
    def _():
        x = jnp.concatenate([xs_ref[pl.ds(t, rows, stride=XE_PITCH), :] for t in range(ROW_TILES)], axis=1).astype(BF16)
        wts = xs_ref[pl.ds(ROW_TILES, rows, stride=XE_PITCH), :]

        pre = [(jnp.dot(x, wg_ref[0], preferred_element_type=F32), jnp.dot(x, wu_ref[0], preferred_element_type=F32))
               for wg_ref, wu_ref in ((wga_ref, wua_ref), (wgb_ref, wub_ref))]
        hid = [((gate / (1.0 + jnp.exp(-gate))) * up).astype(BF16) for gate, up in pre]
        y = (wts[:, 0:1] * jnp.dot(hid[0], wda_ref[0], preferred_element_type=F32)
             + wts[:, 1:2] * jnp.dot(hid[1], wdb_ref[0], preferred_element_type=F32))
        for t in range(ROW_TILES):
            ys_ref[pl.ds(t, rows, stride=ROW_TILES), :] = y[:, t * LANES:(t + 1) * LANES]

    @pl.when(j >= nu_ref[0])
    def _():
        ys_ref[...] = jnp.zeros(ys_ref.shape, F32)


def _ffn(blk_a, blk_b, n_used, xs, wg, wu, wd):
    n_blk = xs.shape[0] // (MOE_ROWS * XE_PITCH)
    up_a = pl.BlockSpec((1, D_MODEL, EXPERT_HIDDEN), lambda j, a, b, nu: (a[j], 0, 0))
    up_b = pl.BlockSpec((1, D_MODEL, EXPERT_HIDDEN), lambda j, a, b, nu: (b[j], 0, 0))
    dn_a = pl.BlockSpec((1, EXPERT_HIDDEN, D_MODEL), lambda j, a, b, nu: (a[j], 0, 0))
    dn_b = pl.BlockSpec((1, EXPERT_HIDDEN, D_MODEL), lambda j, a, b, nu: (b[j], 0, 0))
    return pl.pallas_call(
        _ffn_kernel,
        grid_spec=pltpu.PrefetchScalarGridSpec(
            num_scalar_prefetch=3, grid=(n_blk,),
            in_specs=[pl.BlockSpec((MOE_ROWS * XE_PITCH, LANES), lambda j, a, b, nu: (j, 0)),
                      up_a, up_a, dn_a, up_b, up_b, dn_b],
            out_specs=pl.BlockSpec((MOE_ROWS * ROW_TILES, LANES), lambda j, a, b, nu: (j, 0))),
        out_shape=jax.ShapeDtypeStruct((n_blk * MOE_ROWS * ROW_TILES, LANES), F32),
        compiler_params=_params(("arbitrary",)),
    )(blk_a, blk_b, n_used, xs, wg, wu, wd, wg, wu, wd)


def _combine_kernel(ps_ref, cls_hbm, rank_hbm, ys_hbm, h_ref, g_ref, y_ref, cls_s, rank_s, ybuf, sem, *, tm):
    i = pl.program_id(0)
    n = pl.num_programs(0)

    def idx_copies(t):
        sl = t % 2
        return (pltpu.make_async_copy(cls_hbm.at[pl.ds(t * tm, tm)], cls_s.at[pl.ds(sl * tm, tm)], sem.at[0, sl]),
                pltpu.make_async_copy(rank_hbm.at[pl.ds(t * tm, tm)], rank_s.at[pl.ds(sl * tm, tm)], sem.at[1, sl]))

    def row_copy(sl, r, d):
        return pltpu.make_async_copy(ys_hbm.at[pl.ds(d * ROW_TILES, ROW_TILES)],
                                     ybuf.at[pl.ds((sl * tm + r) * ROW_TILES, ROW_TILES)], sem.at[2, sl])

    def issue_rows(t):
        sl = t % 2

        def body(r, carry):
            row_copy(sl, r, ps_ref[cls_s[sl * tm + r]] + rank_s[sl * tm + r]).start()
            return carry
        lax.fori_loop(0, tm, body, 0, unroll=8)

    @pl.when(i == 0)
    def _():
        for cp in idx_copies(0):
            cp.start()
        for cp in idx_copies(0):
            cp.wait()
        issue_rows(0)

        @pl.when(n > 1)
        def _():
            for cp in idx_copies(1):
                cp.start()

    @pl.when(i + 1 < n)
    def _():
        for cp in idx_copies(i + 1):
            cp.wait()
        issue_rows(i + 1)

    @pl.when(i + 2 < n)
    def _():
        for cp in idx_copies(i + 2):
            cp.start()

    slot = i % 2

    def wait_body(r, carry):
        row_copy(slot, r, 0).wait()
        return carry
    lax.fori_loop(0, tm, wait_body, 0, unroll=8)
    base = slot * tm * ROW_TILES
    y = jnp.concatenate([ybuf[pl.ds(base + t, tm, stride=ROW_TILES), :] for t in range(ROW_TILES)], axis=1)
    y_ref[...] = _rms(h_ref[...] + y, g_ref[...])


def _combine(pstart, cls, rank, ys, h1, g):
    n = h1.shape[0]
    tm = _tile(n, 512)
    any_spec = pl.BlockSpec(memory_space=pl.ANY)
    return pl.pallas_call(
        functools.partial(_combine_kernel, tm=tm),
        grid_spec=pltpu.PrefetchScalarGridSpec(
            num_scalar_prefetch=1, grid=(n // tm,),
            in_specs=[any_spec, any_spec, any_spec, pl.BlockSpec((tm, D_MODEL), lambda i, ps: (i, 0)),
                      pl.BlockSpec((1, D_MODEL), lambda i, ps: (0, 0))],
            out_specs=pl.BlockSpec((tm, D_MODEL), lambda i, ps: (i, 0)),
            scratch_shapes=[pltpu.SMEM((2 * tm,), I32), pltpu.SMEM((2 * tm,), I32),
                            pltpu.VMEM((2 * tm * ROW_TILES, LANES), F32), pltpu.SemaphoreType.DMA((3, 2))]),
        out_shape=jax.ShapeDtypeStruct((n, D_MODEL), F32),
        compiler_params=_params(("arbitrary",)),
    )(pstart, cls, rank, ys, h1, g)


def _moe_and_final(pool_o, attn_o, x, wo, g_ffn, wr, br, wg, wu, wd, g_final):
    n = x.shape[0]
    h1, xe, meta, counts = _outproj_router(pool_o, attn_o, x, wo, g_ffn, wr, br, jnp.zeros((1, LANES), F32))
    cnt = counts[0].astype(I32)
    padded = ((cnt + MOE_ROWS - 1) // MOE_ROWS) * MOE_ROWS
    pends = jnp.cumsum(padded)
    pstart = (pends - padded).astype(I32)
    n_blk = -(-n // MOE_ROWS) + N_CLASSES
    n_used = (pends[-1] // MOE_ROWS).astype(I32)
    blk = jnp.arange(n_blk, dtype=I32)
    blk_last = jnp.minimum(blk, jnp.maximum(n_used - 1, 0))
    blk_cls = jnp.sum((pends[None, :N_CLASSES] <= (blk_last * MOE_ROWS)[:, None]).astype(I32), axis=1)
    blk_cls = jnp.minimum(blk_cls, N_CLASSES - 1)
    blk_a = jnp.asarray(CLASS_A)[blk_cls]
    blk_b = jnp.asarray(CLASS_B)[blk_cls]
    cls, rank = meta[0], meta[1]
    xs = _dispatch(pstart, cls, rank, xe, jnp.zeros((n_blk * MOE_ROWS * XE_PITCH, LANES), F32))
    ys = _ffn(blk_a, blk_b, n_used.reshape(1), xs, wg, wu, wd)
    return _combine(pstart, cls, rank, ys, h1, g_final)


def kernel(x_prompt, x_sample, cache_k, cache_v, cache_idx_k, state_pool, page_table, norm_mix_g, w_in, pool_w,
           pool_scale, w_out, norm_ffn_g, router_group_w, router_group_b, router_expert_w, router_expert_b,
           expert_w_gate, expert_w_up, expert_w_down, final_norm_g):
    depth = w_in.shape[0]
    assert depth == 1, "single-layer step"
    b, s, d = x_prompt.shape
    db, t_new, _ = x_sample.shape
    n_pages = page_table.shape[1]
    past = n_pages * PAGE_SIZE
    n_pool = cache_k.shape[1]
    l = 0

    w_in_b = jnp.pad(w_in[l], ((0, 0), (0, IN_PAD - IN_WIDTH))).astype(BF16)
    g_mix = norm_mix_g[l].reshape(1, d)
    pw = pool_w[l].astype(BF16)
    psc = pool_scale[l].reshape(1, POOL_WIDTH)
    wo = w_out[l].astype(BF16)
    g_ffn = norm_ffn_g[l].reshape(1, d)
    wr = jnp.pad(jnp.concatenate([router_group_w[l], router_expert_w[l]], axis=1),
                 ((0, 0), (0, LANES - N_GROUPS - N_EXPERTS))).astype(BF16)
    br = jnp.pad(jnp.concatenate([router_group_b[l], router_expert_b[l]]), (0, LANES - N_GROUPS - N_EXPERTS)).reshape(1, LANES)
    wg = expert_w_gate[l].astype(BF16)
    wu = expert_w_up[l].astype(BF16)
    wd = expert_w_down[l].astype(BF16)
    g_fin = final_norm_g.reshape(1, d)
    moe = lambda pool_o, attn_o, x: _moe_and_final(pool_o, attn_o, x, wo, g_ffn, wr, br, wg, wu, wd, g_fin)

    xp = x_prompt.reshape(b * s, d)
    tab_p = _rope_tables(jnp.arange(s, dtype=I32))
    u, qt, k, v, qit, ki, kw, kb, vt, kwb = _inproj(xp, g_mix, w_in_b, tab_p, seq_len=s)
    r3 = lambda a: a.reshape(b, s, a.shape[-1])
    pool_o = _pool_prompt(r3(u), pw, psc)
    attn_o = _dsa_prompt(qt, qit, r3(kw), r3(kwb), r3(kb), vt, min(TOPK_MAX, s // 4))
    y_prompt = moe(pool_o.reshape(b * s, POOL_WIDTH), attn_o.reshape(b * s, ATTN_WIDTH), xp).reshape(b, s, d)
    k_prompt = jnp.moveaxis(k.reshape(b, N_KV_HEADS, HEAD_DIM, s), -1, 1)[None]
    v_prompt = jnp.moveaxis(v.reshape(b, N_KV_HEADS, HEAD_DIM, s), -1, 1)[None]
    idx_k_prompt = jnp.moveaxis(ki, -1, 1)[None]
    if s >= POOL_STATE:
        pool_prompt = r3(u)[:, s - POOL_STATE:][None]
    else:
        pool_prompt = jnp.concatenate([jnp.zeros((b, POOL_STATE, POOL_WIDTH), F32), r3(u)], axis=1)[:, -POOL_STATE:][None]

    xs_ = x_sample.reshape(db * t_new, d)
    tab_s = _rope_tables(past + (jnp.arange(db * t_new, dtype=I32) % t_new))
    u2, q2, k2, v2, qi2, ki2, kw2, _, _, _ = _inproj(xs_, g_mix, w_in_b, tab_s)
    u2_3 = u2.reshape(db, t_new, POOL_WIDTH)
    pool2 = _pool_sample(jnp.swapaxes(state_pool[l], 0, 1), jnp.swapaxes(u2_3, 0, 1), pw, psc, past)
    pool2 = jnp.swapaxes(pool2, 0, 1).reshape(db * t_new, POOL_WIDTH)
    page_t = lambda c: jnp.moveaxis(c[l], 1, -1).reshape(n_pool, -1, PAGE_SIZE)
    attn2 = _dsa_sample(page_table, q2, qi2, kw2, k2, v2, page_t(cache_idx_k), page_t(cache_k), page_t(cache_v),
                        min(TOPK_MAX, (past + t_new) // 4))
    y_sample = moe(pool2, attn2, xs_).reshape(db, t_new, d)
    k_sample = k2.reshape(1, db, t_new, N_KV_HEADS, HEAD_DIM)
    v_sample = v2.reshape(1, db, t_new, N_KV_HEADS, HEAD_DIM)
    idx_k_sample = ki2.reshape(1, db, t_new, IDX_DIM)
    pool_sample = jnp.concatenate([state_pool[l], u2_3], axis=1)[:, -POOL_STATE:][None]

    return (y_prompt, y_sample, k_prompt, v_prompt, idx_k_prompt, pool_prompt,
            k_sample, v_sample, idx_k_sample, pool_sample)
```
